```python
import jax, jax.numpy as jnp
from jax import lax
import numpy as np

D_MODEL = 4096
BATCH = 4
SEQ = 2048
DEPTH = 1
DEC_BATCH = 128
DEC_SEQ = 8
PAST_LEN = 16384
PAGE_SIZE = 128

D_MIX = D_MODEL
D_A = D_MIX // 2
HEAD_A = 128
H_A = D_A // HEAD_A
CHUNK = 128
D_M = D_MIX - D_A
V_DIM = 128
H_M = D_M // V_DIM
NOPE_DIM = 128
ROPE_DIM = 64
QK_DIM = NOPE_DIM + ROPE_DIM
Q_LORA = D_MODEL // 4
KV_LORA = 512
ROPE_THETA = 10000.0
IN_COLS = 2 * D_A + Q_LORA + KV_LORA + ROPE_DIM
D_FF = ((8 * D_MODEL // 3 + 255) // 256) * 256
N_MOD = 9
ATTN_BLOCK = 128
EPS = 1e-6

kernel_name = 'hymba_gmlp_mla_macaron_adaln_step'


def rms_norm(x, g):
    xf = x.astype(jnp.float32)
    y = xf * lax.rsqrt(jnp.mean(xf * xf, axis=-1, keepdims=True) + EPS)
    return (y * g.astype(jnp.float32)).astype(x.dtype)


def ada_norm(x, g, shift, scale):
    xf = x.astype(jnp.float32)
    y = xf * lax.rsqrt(jnp.mean(xf * xf, axis=-1, keepdims=True) + EPS) * g.astype(jnp.float32)
    return (y * (1.0 + scale.astype(jnp.float32)) + shift.astype(jnp.float32)).astype(x.dtype)


def rope(x, pos):
    half = ROPE_DIM // 2
    inv_freq = ROPE_THETA ** (-jnp.arange(half, dtype=jnp.float32) / half)
    ang = pos.astype(jnp.float32)[:, None] * inv_freq[None, :]
    cos = jnp.cos(ang)[:, None, :]
    sin = jnp.sin(ang)[:, None, :]
    xf = x.astype(jnp.float32)
    x1, x2 = xf[..., :half], xf[..., half:]
    return jnp.concatenate([x1 * cos - x2 * sin, x2 * cos + x1 * sin], axis=-1).astype(x.dtype)


def swiglu(h, w_gate, w_up, w_down):
    return (jax.nn.silu(h @ w_gate) * (h @ w_up)) @ w_down


def chunk_mix(u, v, w_s, b_s):
    B, T = u.shape[0], u.shape[1]
    pad = (-T) % CHUNK
    vp = jnp.pad(v, ((0, 0), (0, pad), (0, 0), (0, 0)))
    n = (T + pad) // CHUNK
    vc = vp.reshape(B, n, CHUNK, H_A, HEAD_A)
    causal = jnp.tril(jnp.ones((CHUNK, CHUNK), dtype=bool))
    w = jnp.where(causal[None], w_s, 0.0).astype(v.dtype)
    mixed = jnp.einsum('hts,bnshd->bnthd', w, vc) + b_s.T.astype(v.dtype)[None, None, :, :, None]
    mixed = mixed.reshape(B, n * CHUNK, H_A, HEAD_A)[:, :T]
    return u * mixed


def expand_kv(ckv, kr, w_kv_b, g_k_norm):
    kv = jnp.einsum('...tc,chd->...thd', ckv, w_kv_b)
    k_nope, v = kv[..., :NOPE_DIM], kv[..., NOPE_DIM:]
    k_rope = jnp.broadcast_to(kr[..., None, :], k_nope.shape[:-1] + (ROPE_DIM,))
    k = rms_norm(jnp.concatenate([k_nope, k_rope], axis=-1), g_k_norm)
    return k, v


def attend(q, k, v, q_pos, k_pos):
    s = jnp.einsum('bqhd,bkhd->bhqk', q, k).astype(jnp.float32) * (QK_DIM ** -0.5)
    s = jnp.where((k_pos[None, :] <= q_pos[:, None])[None, None], s, -jnp.inf)
    p = jax.nn.softmax(s, axis=-1).astype(v.dtype)
    return jnp.einsum('bhqk,bkhd->bqhd', p, v)


def prompt_attention(q, k, v):
    B, T = q.shape[0], q.shape[1]
    nb = T // ATTN_BLOCK
    qb = q.reshape(B, nb, ATTN_BLOCK, H_M, QK_DIM).transpose(1, 0, 2, 3, 4)
    pos = jnp.arange(T)
    qpos = pos.reshape(nb, ATTN_BLOCK)
    out = lax.map(lambda a: attend(a[0], k, v, a[1], pos), (qb, qpos))
    return out.transpose(1, 0, 2, 3, 4).reshape(B, T, H_M, V_DIM)


def sample_attention(q, ckv_new, kr_new, cache_ckv, cache_krope, page_table, layer, w_kv_b, g_k_norm):
    past = page_table.shape[1] * PAGE_SIZE
    t_new = q.shape[1]
    k_pos = jnp.arange(past + t_new)
    q_pos = past + jnp.arange(t_new)

    def one(args):
        pages, q_b, c_b, r_b = args
        c_all = jnp.concatenate([cache_ckv[layer, pages].reshape(past, KV_LORA).astype(c_b.dtype), c_b], axis=0)
        r_all = jnp.concatenate([cache_krope[layer, pages].reshape(past, ROPE_DIM).astype(r_b.dtype), r_b], axis=0)
        k, v = expand_kv(c_all, r_all, w_kv_b, g_k_norm)
        return attend(q_b[None], k[None], v[None], q_pos, k_pos)[0]

    return lax.map(one, (page_table, q, ckv_new, kr_new))


def decoder_layer(x, c, pos, attn_fn, p):
    B, T = x.shape[0], x.shape[1]
    mod = (jax.nn.silu(c) @ p['w_ada'] + p['b_ada']).reshape(B, N_MOD, D_MODEL)[:, :, None, :]
    sh1, sc1, gt1 = mod[:, 0], mod[:, 1], mod[:, 2]
    sh2, sc2, gt2 = mod[:, 3], mod[:, 4], mod[:, 5]
    sh3, sc3, gt3 = mod[:, 6], mod[:, 7], mod[:, 8]

    x = x + 0.5 * gt1 * swiglu(ada_norm(x, p['g_norm1'], sh1, sc1), p['w_ff1_gate'], p['w_ff1_up'], p['w_ff1_down'])

    h = ada_norm(x, p['g_norm2'], sh2, sc2)
    proj = h @ p['w_in']
    u_raw, v_raw, cq, ckv, kr = jnp.split(
        proj, [D_A, 2 * D_A, 2 * D_A + Q_LORA, 2 * D_A + Q_LORA + KV_LORA], axis=-1)

    u = jax.nn.gelu(u_raw).reshape(B, T, H_A, HEAD_A)
    v_a = rms_norm(jax.nn.gelu(v_raw).reshape(B, T, H_A, HEAD_A), p['g_v_a'])
    a_out = chunk_mix(u, v_a, p['w_s'], p['b_s']).reshape(B, T, D_A)

    q = jnp.einsum('btc,chd->bthd', rms_norm(cq, p['g_q_a']), p['w_q_b'])
    q = jnp.concatenate([q[..., :NOPE_DIM], rope(q[..., NOPE_DIM:], pos)], axis=-1)
    q = rms_norm(q, p['g_q_norm'])
    ckv = rms_norm(ckv, p['g_kv_a'])
    kr = rope(kr[:, :, None, :], pos)[:, :, 0, :]
    m_out = attn_fn(q, ckv, kr).reshape(B, T, D_M)

    mix = jnp.concatenate([rms_norm(a_out, p['g_out_a']), rms_norm(m_out, p['g_out_m'])], axis=-1) @ p['w_out']
    x = x + gt2 * mix

    x = x + 0.5 * gt3 * swiglu(ada_norm(x, p['g_norm3'], sh3, sc3), p['w_ff2_gate'], p['w_ff2_up'], p['w_ff2_down'])
    return x, ckv, kr, v_a


def setup_inputs(seed: int = 0) -> dict:
    key = jax.random.key(seed)
    ks = iter(jax.random.split(key, 40))
    f32 = jnp.float32

    def nrm(shape, scale):
        return jax.random.normal(next(ks), shape, f32) * scale

    def gain(shape):
        return 1.0 + nrm(shape, 0.02)

    n_pages = PAST_LEN // PAGE_SIZE
    n_used = DEC_BATCH * n_pages
    n_pool = n_used + max(1, n_used // 4)
    x_prompt = nrm((BATCH, SEQ, D_MODEL), 1.0)
    x_sample = nrm((DEC_BATCH, DEC_SEQ, D_MODEL), 1.0)
    cache_ckv = nrm((DEPTH, n_pool, PAGE_SIZE, KV_LORA), 1.0)
    cache_krope = nrm((DEPTH, n_pool, PAGE_SIZE, ROPE_DIM), 1.0)
    page_table = jax.random.permutation(next(ks), n_pool)[:n_used].reshape(DEC_BATCH, n_pages).astype(jnp.int32)
    c_prompt = nrm((BATCH, D_MODEL), 1.0)
    c_sample = nrm((DEC_BATCH, D_MODEL), 1.0)
    L = DEPTH
    return {
        'x_prompt': x_prompt,
        'x_sample': x_sample,
        'cache_ckv': cache_ckv,
        'cache_krope': cache_krope,
        'page_table': page_table,
        'c_prompt': c_prompt,
        'c_sample': c_sample,
        'w_ada': nrm((L, D_MODEL, N_MOD * D_MODEL), D_MODEL ** -0.5),
        'b_ada': nrm((L, N_MOD * D_MODEL), 0.02),
        'g_norm1': gain((L, D_MODEL)),
        'w_ff1_gate': nrm((L, D_MODEL, D_FF), D_MODEL ** -0.5),
        'w_ff1_up': nrm((L, D_MODEL, D_FF), D_MODEL ** -0.5),
        'w_ff1_down': nrm((L, D_FF, D_MODEL), D_FF ** -0.5),
        'g_norm2': gain((L, D_MODEL)),
        'w_in': nrm((L, D_MODEL, IN_COLS), D_MODEL ** -0.5),
        'w_s': nrm((L, H_A, CHUNK, CHUNK), CHUNK ** -0.5),
        'b_s': 1.0 + nrm((L, H_A, CHUNK), 0.1),
        'g_v_a': gain((L, HEAD_A)),
        'g_q_a': gain((L, Q_LORA)),
        'w_q_b': nrm((L, Q_LORA, H_M, QK_DIM), Q_LORA ** -0.5),
        'g_kv_a': gain((L, KV_LORA)),
        'w_kv_b': nrm((L, KV_LORA, H_M, NOPE_DIM + V_DIM), KV_LORA ** -0.5),
        'g_q_norm': gain((L, QK_DIM)),
        'g_k_norm': gain((L, QK_DIM)),
        'g_out_a': gain((L, D_A)),
        'g_out_m': gain((L, D_M)),
        'w_out': nrm((L, D_MIX, D_MODEL), D_MIX ** -0.5),
        'g_norm3': gain((L, D_MODEL)),
        'w_ff2_gate': nrm((L, D_MODEL, D_FF), D_MODEL ** -0.5),
        'w_ff2_up': nrm((L, D_MODEL, D_FF), D_MODEL ** -0.5),
        'w_ff2_down': nrm((L, D_FF, D_MODEL), D_FF ** -0.5),
    }


def reference(x_prompt, x_sample, cache_ckv, cache_krope, page_table, c_prompt, c_sample,
              w_ada, b_ada, g_norm1, w_ff1_gate, w_ff1_up, w_ff1_down, g_norm2, w_in,
              w_s, b_s, g_v_a, g_q_a, w_q_b, g_kv_a, w_kv_b, g_q_norm, g_k_norm,
              g_out_a, g_out_m, w_out, g_norm3, w_ff2_gate, w_ff2_up, w_ff2_down):
    pos_p = jnp.arange(x_prompt.shape[1])
    n_past = page_table.shape[1] * PAGE_SIZE
    pos_s = n_past + jnp.arange(x_sample.shape[1])
    yp, ys = x_prompt, x_sample
    ckv_p, kr_p, ckv_s, kr_s, va_s = [], [], [], [], []
    for l in range(DEPTH):
        p = {
            'w_ada': w_ada[l], 'b_ada': b_ada[l], 'g_norm1': g_norm1[l],
            'w_ff1_gate': w_ff1_gate[l], 'w_ff1_up': w_ff1_up[l], 'w_ff1_down': w_ff1_down[l],
            'g_norm2': g_norm2[l], 'w_in': w_in[l], 'w_s': w_s[l], 'b_s': b_s[l], 'g_v_a': g_v_a[l],
            'g_q_a': g_q_a[l], 'w_q_b': w_q_b[l], 'g_kv_a': g_kv_a[l], 'w_kv_b': w_kv_b[l],
            'g_q_norm': g_q_norm[l], 'g_k_norm': g_k_norm[l], 'g_out_a': g_out_a[l], 'g_out_m': g_out_m[l],
            'w_out': w_out[l], 'g_norm3': g_norm3[l],
            'w_ff2_gate': w_ff2_gate[l], 'w_ff2_up': w_ff2_up[l], 'w_ff2_down': w_ff2_down[l],
        }
        w_kvb_l, g_kn_l = p['w_kv_b'], p['g_k_norm']
        yp, c1, r1, _ = decoder_layer(
            yp, c_prompt, pos_p,
            lambda q, c, r: prompt_attention(q, *expand_kv(c, r, w_kvb_l, g_kn_l)), p)
        ys, c2, r2, v2 = decoder_layer(
            ys, c_sample, pos_s,
            lambda q, c, r: sample_attention(q, c, r, cache_ckv, cache_krope, page_table, l, w_kvb_l, g_kn_l), p)
        ckv_p.append(c1)
        kr_p.append(r1)
        ckv_s.append(c2)
        kr_s.append(r2)
        va_s.append(v2)
    return (yp, ys, jnp.stack(ckv_p), jnp.stack(kr_p), jnp.stack(ckv_s), jnp.stack(kr_s), jnp.stack(va_s))
```

```python
import functools

import jax
import jax.numpy as jnp
from jax import lax
from jax.experimental import pallas as pl
from jax.experimental.pallas import tpu as pltpu

F32 = jnp.float32
BF16 = jnp.bfloat16

D_MODEL = 4096
D_A = 2048
HEAD_A = 128
H_A = 16
CHUNK = 128
D_M = 2048
V_DIM = 128
H_M = 16
NOPE_DIM = 128
ROPE_DIM = 64
HALF_ROPE = ROPE_DIM // 2
QK_DIM = NOPE_DIM + ROPE_DIM
Q_LORA = 1024
KV_LORA = 512
ROPE_THETA = 10000.0
D_FF = 11008
N_MOD = 9
PAGE_SIZE = 128
EPS = 1e-6

QH_W = 256
QA_W = KV_LORA + 128
LANES = 128
MIB = 1024 * 1024
ROW_STEP = 16
FFN_DOWN_COLS = 512

_NT = (((1,), (1,)), ((), ()))
_TN = (((0,), (0,)), ((), ()))


def _dot(a, b):
    return jnp.dot(a, b, preferred_element_type=F32)


def _params(semantics, vmem_mib):
    return pltpu.CompilerParams(dimension_semantics=semantics, vmem_limit_bytes=vmem_mib * MIB)


def _single(block_shape, index_map):
    return pl.BlockSpec(block_shape, index_map, pipeline_mode=pl.Buffered(1))


def _ada_norm(x, g, sh, sc):
    r = lax.rsqrt(jnp.mean(x * x, axis=-1, keepdims=True) + EPS)
    return (x * r * g) * (1.0 + sc) + sh


def _ada_norm_rows(h_ref, x_ref, g_ref, sh_ref, sc_ref):
    per_row = sh_ref.shape[0] != 1

    def body(i, carry):
        rows = pl.ds(pl.multiple_of(i * ROW_STEP, ROW_STEP), ROW_STEP)
        sh = sh_ref[rows, :] if per_row else sh_ref[...]
        sc = sc_ref[rows, :] if per_row else sc_ref[...]
        h_ref[rows, :] = _ada_norm(x_ref[rows, :], g_ref[...], sh, sc).astype(BF16)
        return carry

    lax.fori_loop(0, x_ref.shape[0] // ROW_STEP, body, 0)


def _rms_scale(x, width):
    return lax.rsqrt(jnp.sum(x * x, axis=-1, keepdims=True) * (1.0 / width) + EPS)


def _gelu(x):
    return jax.nn.gelu(x)


class _Tokens:
    def __init__(self, sample, n_tok, seq_len, mod):
        self.sample = sample
        self.n_tok = n_tok
        self.seq_len = seq_len
        self.mod = mod

    def mod_spec(self, tm, k):
        if self.sample:
            return pl.BlockSpec((tm, D_MODEL), lambda i, *_: (i, k))
        per_seq = self.seq_len // tm
        return pl.BlockSpec((None, 1, D_MODEL), lambda i, *_: ((i // per_seq) * N_MOD + k, 0, 0))

    def pos_spec(self, tm, width):
        if self.sample:
            return pl.BlockSpec((tm, width), lambda i, *_: (i, 0))
        per_seq = self.seq_len // tm
        return pl.BlockSpec((tm, width), lambda i, *_: (i % per_seq, 0))

    def tile(self, want):
        tm = min(want, 256, self.n_tok) if self.sample else min(want, self.seq_len)
        assert self.n_tok % tm == 0 and (self.sample or self.seq_len % tm == 0)
        return tm


def _mod_kernel(c_ref, w_ref, b_ref, o_ref):
    c = c_ref[...]
    a = (c * jax.nn.sigmoid(c)).astype(BF16)
    o_ref[...] = _dot(a, w_ref[...].astype(BF16)) + b_ref[...]


def _mod_call(c, w_ada, b_ada):
    m, n = c.shape[0], w_ada.shape[1]
    tn = 1024
    return pl.pallas_call(
        _mod_kernel,
        grid=(n // tn,),
        in_specs=[
            pl.BlockSpec((m, D_MODEL), lambda j: (0, 0)),
            pl.BlockSpec((D_MODEL, tn), lambda j: (0, j)),
            pl.BlockSpec((1, tn), lambda j: (0, j)),
        ],
        out_specs=pl.BlockSpec((m, tn), lambda j: (0, j)),
        out_shape=jax.ShapeDtypeStruct((m, n), F32),
        compiler_params=_params(("arbitrary",), 56),
        name="mod",
    )(c, w_ada, b_ada)


def _ffn_kernel(x_ref, sh_ref, sc_ref, gt_ref, g_ref, wg_ref, wu_ref, wd_ref, o_ref, h_ref):
    k = pl.program_id(1)

    @pl.when(k == 0)
    def _():
        _ada_norm_rows(h_ref, x_ref, g_ref, sh_ref, sc_ref)
        o_ref[...] = jnp.zeros_like(o_ref)

    h = h_ref[...]
    g = _dot(h, wg_ref[...])
    u = _dot(h, wu_ref[...])
    a = (g * jax.nn.sigmoid(g) * u).astype(BF16)
    for n0 in range(0, D_MODEL, FFN_DOWN_COLS):
        cols = slice(n0, n0 + FFN_DOWN_COLS)
        o_ref[:, cols] += _dot(a, wd_ref[:, cols])

    @pl.when(k == pl.num_programs(1) - 1)
    def _():
        o_ref[...] = x_ref[...] + (0.5 * gt_ref[...]) * o_ref[...]


def _ffn_call(tok, x, mod_base, g_norm, wg, wu, wd):
    tm = tok.tile(1024)
    tf = 256
    n_ff = wg.shape[1]
    return pl.pallas_call(
        _ffn_kernel,
        grid=(tok.n_tok // tm, n_ff // tf),
        in_specs=[
            _single((tm, D_MODEL), lambda i, k: (i, 0)),
            tok.mod_spec(tm, mod_base),
            tok.mod_spec(tm, mod_base + 1),
            tok.mod_spec(tm, mod_base + 2),
            pl.BlockSpec((1, D_MODEL), lambda i, k: (0, 0)),
            pl.BlockSpec((D_MODEL, tf), lambda i, k: (0, k)),
            pl.BlockSpec((D_MODEL, tf), lambda i, k: (0, k)),
            pl.BlockSpec((tf, D_MODEL), lambda i, k: (k, 0)),
        ],
        out_specs=_single((tm, D_MODEL), lambda i, k: (i, 0)),
        out_shape=jax.ShapeDtypeStruct((tok.n_tok, D_MODEL), F32),
        scratch_shapes=[pltpu.VMEM((tm, D_MODEL), BF16)],
        compiler_params=_params(("arbitrary", "arbitrary"), 60),
        name="ffn",
    )(x, tok.mod, tok.mod, tok.mod, g_norm, wg, wu, wd)


def _uv_kernel(x_ref, sh_ref, sc_ref, g_ref, wu_ref, wv_ref, gv_ref, ws_ref, bs_ref, *rest, sample):
    if sample:
        a_ref, va_ref, h_ref = rest
    else:
        a_ref, h_ref = rest
    j = pl.program_id(1)

    @pl.when(j == 0)
    def _():
        _ada_norm_rows(h_ref, x_ref, g_ref, sh_ref, sc_ref)

    h = h_ref[...]
    tm = h.shape[0]
    u = _gelu(_dot(h, wu_ref[...]))
    v = _gelu(_dot(h, wv_ref[...]))
    for hh in range(2):
        cols = slice(hh * HEAD_A, (hh + 1) * HEAD_A)
        vh = v[:, cols]
        vh = vh * _rms_scale(vh, HEAD_A) * gv_ref[...]
        if sample:
            va_ref[:, cols] = vh
            nb = tm // 8
            v3 = vh.reshape(nb, 8, HEAD_A)
            mixed = jnp.broadcast_to(bs_ref[:, cols][None], (nb, 8, HEAD_A))
            for s in range(8):
                row = jnp.broadcast_to(v3[:, s:s + 1, :], (nb, 8, HEAD_A))
                mixed = mixed + row * ws_ref[s, :, cols][None]
            a_ref[:, cols] = u[:, cols] * mixed.reshape(tm, HEAD_A)
        else:
            w = ws_ref[hh]
            vb = vh.astype(BF16)
            for c in range(tm // CHUNK):
                rows = slice(c * CHUNK, (c + 1) * CHUNK)
                mixed = _dot(w, vb[rows]) + bs_ref[hh]
                a_ref[rows, cols] = u[rows, cols] * mixed


def _uv_call(tok, x, g_norm, wu, wv, g_v, ws, bs):
    tm = tok.tile(1024)
    sample = tok.sample
    if sample:
        ws_spec = pl.BlockSpec((8, 8, 2 * HEAD_A), lambda i, j: (0, 0, j))
        bs_spec = pl.BlockSpec((8, 2 * HEAD_A), lambda i, j: (0, j))
    else:
        assert tm % CHUNK == 0
        ws_spec = pl.BlockSpec((2, CHUNK, CHUNK), lambda i, j: (j, 0, 0))
        bs_spec = pl.BlockSpec((2, CHUNK, HEAD_A), lambda i, j: (j, 0, 0))
    blk = pl.BlockSpec((tm, 2 * HEAD_A), lambda i, j: (i, j))
    out_shape = [jax.ShapeDtypeStruct((tok.n_tok, D_A), F32)]
    out_specs = [blk]
    if sample:
        out_shape.append(jax.ShapeDtypeStruct((tok.n_tok, D_A), F32))
        out_specs.append(blk)
    return pl.pallas_call(
        functools.partial(_uv_kernel, sample=sample),
        grid=(tok.n_tok // tm, H_A // 2),
        in_specs=[
            _single((tm, D_MODEL), lambda i, j: (i, 0)),
            tok.mod_spec(tm, 3),
            tok.mod_spec(tm, 4),
            pl.BlockSpec((1, D_MODEL), lambda i, j: (0, 0)),
            pl.BlockSpec((D_MODEL, 2 * HEAD_A), lambda i, j: (0, j)),
            pl.BlockSpec((D_MODEL, 2 * HEAD_A), lambda i, j: (0, j)),
            pl.BlockSpec((1, HEAD_A), lambda i, j: (0, 0)),
            ws_spec,
            bs_spec,
        ],
        out_specs=out_specs,
        out_shape=out_shape,
        scratch_shapes=[pltpu.VMEM((tm, D_MODEL), BF16)],
        compiler_params=_params(("arbitrary", "arbitrary"), 48),
        name="uv",
    )(x, tok.mod, tok.mod, g_norm, wu, wv, g_v, ws, bs)


def _qkr_kernel(x_ref, sh_ref, sc_ref, g_ref, wb_ref, gq_ref, gkv_ref, tk_ref, cq_ref, ckv_ref, kr_ref,
                h_ref):
    _ada_norm_rows(h_ref, x_ref, g_ref, sh_ref, sc_ref)
    p = _dot(h_ref[...], wb_ref[...])
    cq = p[:, :Q_LORA]
    cq_ref[...] = (cq * _rms_scale(cq, Q_LORA) * gq_ref[...]).astype(BF16)
    ckv = p[:, Q_LORA:Q_LORA + KV_LORA]
    ckv_ref[...] = ckv * _rms_scale(ckv, KV_LORA) * gkv_ref[...]
    e = p[:, Q_LORA + KV_LORA:] * tk_ref[...]
    kr_ref[...] = e + pltpu.roll(e, 2 * HALF_ROPE, axis=1)


def _qkr_call(tok, x, g_norm, wb, g_q_a, g_kv_a, tk):
    tm = tok.tile(512)
    nb = wb.shape[1]
    row = lambda w: pl.BlockSpec((tm, w), lambda i: (i, 0))
    return pl.pallas_call(
        _qkr_kernel,
        grid=(tok.n_tok // tm,),
        in_specs=[
            row(D_MODEL),
            tok.mod_spec(tm, 3),
            tok.mod_spec(tm, 4),
            pl.BlockSpec((1, D_MODEL), lambda i: (0, 0)),
            _single((D_MODEL, nb), lambda i: (0, 0)),
            pl.BlockSpec((1, Q_LORA), lambda i: (0, 0)),
            pl.BlockSpec((1, KV_LORA), lambda i: (0, 0)),
            tok.pos_spec(tm, LANES),
        ],
        out_specs=[row(Q_LORA), row(KV_LORA), row(LANES)],
        out_shape=[
            jax.ShapeDtypeStruct((tok.n_tok, Q_LORA), BF16),
            jax.ShapeDtypeStruct((tok.n_tok, KV_LORA), F32),
            jax.ShapeDtypeStruct((tok.n_tok, LANES), F32),
        ],
        scratch_shapes=[pltpu.VMEM((tm, D_MODEL), BF16)],
        compiler_params=_params(("arbitrary",), 56),
        name="qkr",
    )(x, tok.mod, tok.mod, g_norm, wb, g_q_a, g_kv_a, tk)


def _q_heads(cq_ref, wqb_ref, tq_ref, ga_ref):
    q = _dot(cq_ref[...], wqb_ref[...])
    tq = tq_ref[...]
    ga = ga_ref[...]
    for h in range(H_M):
        a = q[:, h * QH_W:h * QH_W + NOPE_DIM]
        b = q[:, h * QH_W + NOPE_DIM:(h + 1) * QH_W]
        ss = jnp.sum(a * a + 0.5 * (b * b), axis=-1, keepdims=True)
        r = lax.rsqrt(ss * (1.0 / QK_DIM) + EPS) * (QK_DIM ** -0.5)
        yield h, a * ga * r, b * tq * r


def _qpost_p_kernel(cq_ref, ckv_ref, kr_ref, tq_ref, ga_ref, gkn_ref, gkr_ref, wqb_ref, wkvb_ref,
                    q_ref, k_ref, v_ref):
    for h, a, b in _q_heads(cq_ref, wqb_ref, tq_ref, ga_ref):
        q_ref[:, h * QH_W:h * QH_W + NOPE_DIM] = a.astype(BF16)
        q_ref[:, h * QH_W + NOPE_DIM:(h + 1) * QH_W] = b.astype(BF16)
    kv = _dot(ckv_ref[...].astype(BF16), wkvb_ref[...])
    v_ref[...] = kv[:, D_M:].astype(BF16)
    kr = kr_ref[...]
    rr = 0.5 * jnp.sum(kr * kr, axis=-1, keepdims=True)
    krg = kr * gkr_ref[...]
    for h in range(H_M):
        kn = kv[:, h * NOPE_DIM:(h + 1) * NOPE_DIM]
        r = lax.rsqrt((jnp.sum(kn * kn, axis=-1, keepdims=True) + rr) * (1.0 / QK_DIM) + EPS)
        k_ref[:, h * QH_W:h * QH_W + NOPE_DIM] = (kn * gkn_ref[...] * r).astype(BF16)
        k_ref[:, h * QH_W + NOPE_DIM:(h + 1) * QH_W] = (krg * r).astype(BF16)


def _qpost_s_kernel(cq_ref, tq_ref, ga_ref, wqb_ref, wkbt_ref, qa_ref):
    tm = cq_ref.shape[0]
    lane = lax.broadcasted_iota(jnp.int32, (tm, LANES), 1)
    for h, a, b in _q_heads(cq_ref, wqb_ref, tq_ref, ga_ref):
        qabs = _dot(a.astype(BF16), wkbt_ref[h])
        qr = jnp.where(lane < ROPE_DIM, b + pltpu.roll(b, ROPE_DIM, axis=1), 0.0)
        qa_ref[:, h, :, :KV_LORA] = qabs.reshape(tm // 8, 8, KV_LORA)
        qa_ref[:, h, :, KV_LORA:] = qr.reshape(tm // 8, 8, LANES)


def _qpost_call(tok, cq, ckv, kr, tq, ga, gkn, gkr, wqb, wkvb, wkbt):
    tm = tok.tile(256)
    row = lambda w: pl.BlockSpec((tm, w), lambda i: (i, 0))
    one = lambda w: pl.BlockSpec((1, w), lambda i: (0, 0))
    n = tok.n_tok
    if tok.sample:
        return pl.pallas_call(
            _qpost_s_kernel,
            grid=(n // tm,),
            in_specs=[
                row(Q_LORA), tok.pos_spec(tm, LANES), one(NOPE_DIM),
                _single((Q_LORA, H_M * QH_W), lambda i: (0, 0)),
                _single((H_M, NOPE_DIM, KV_LORA), lambda i: (0, 0, 0)),
            ],
            out_specs=pl.BlockSpec((tm // 8, H_M, 8, QA_W), lambda i: (i, 0, 0, 0)),
            out_shape=jax.ShapeDtypeStruct((n // 8, H_M, 8, QA_W), F32),
            compiler_params=_params(("arbitrary",), 56),
            name="qpost_s",
        )(cq, tq, ga, wqb, wkbt)
    return pl.pallas_call(
        _qpost_p_kernel,
        grid=(n // tm,),
        in_specs=[
            row(Q_LORA), row(KV_LORA), row(LANES), tok.pos_spec(tm, LANES),
            one(NOPE_DIM), one(NOPE_DIM), one(LANES),
            _single((Q_LORA, H_M * QH_W), lambda i: (0, 0)),
            _single((KV_LORA, 2 * D_M), lambda i: (0, 0)),
        ],
        out_specs=[row(H_M * QH_W), row(H_M * QH_W), row(D_M)],
        out_shape=[
            jax.ShapeDtypeStruct((n, H_M * QH_W), BF16),
            jax.ShapeDtypeStruct((n, H_M * QH_W), BF16),
            jax.ShapeDtypeStruct((n, D_M), BF16),
        ],
        compiler_params=_params(("arbitrary",), 56),
        name="qpost_p",
    )(cq, ckv, kr, tq, ga, gkn, gkr, wqb, wkvb)


def _attn_p_kernel(q_ref, k_ref, v_ref, o_ref):
    tq, tk = q_ref.shape[0], k_ref.shape[0]
    s = lax.dot_general(q_ref[...], k_ref[...], _NT, preferred_element_type=F32)
    q_pos = pl.program_id(2) * tq + lax.broadcasted_iota(jnp.int32, (tq, tk), 0)
    k_pos = lax.broadcasted_iota(jnp.int32, (tq, tk), 1)
    s = jnp.where(k_pos <= q_pos, s, -jnp.inf)
    e = jnp.exp(s - jnp.max(s, axis=-1, keepdims=True))
    p = e * (1.0 / jnp.sum(e, axis=-1, keepdims=True))
    o_ref[...] = _dot(p.astype(BF16), v_ref[...])


def _attn_p_call(q, k, v, n_seq, seq_len):
    tq = min(512, seq_len)
    nq = seq_len // tq
    return pl.pallas_call(
        _attn_p_kernel,
        grid=(n_seq, H_M, nq),
        in_specs=[
            pl.BlockSpec((tq, QH_W), lambda b, h, i: (b * nq + i, h)),
            pl.BlockSpec((seq_len, QH_W), lambda b, h, i: (b, h)),
            pl.BlockSpec((seq_len, V_DIM), lambda b, h, i: (b, h)),
        ],
        out_specs=pl.BlockSpec((tq, V_DIM), lambda b, h, i: (b * nq + i, h)),
        out_shape=jax.ShapeDtypeStruct((n_seq * seq_len, D_M), F32),
        compiler_params=_params(("arbitrary", "arbitrary", "arbitrary"), 48),
        name="attn_p",
    )(q, k, v)


def _attn_s_kernel(pt_ref, q_ref, cn_ref, rn_ref, wkb_ref, bd_ref, *rest, npb):
    del pt_ref
    c_refs, r_refs = rest[:npb], rest[npb:2 * npb]
    o_ref, cb_ref, new_ref, qf_ref, m_ref, l_ref, acc_ref = rest[2 * npb:]
    j = pl.program_id(1)
    nq = q_ref.shape[1]
    ncol = H_M * nq
    eye = (lax.broadcasted_iota(jnp.int32, (ncol, ncol), 0)
           == lax.broadcasted_iota(jnp.int32, (ncol, ncol), 1))

    def to_col(row):
        return jnp.sum(jnp.where(eye, jnp.broadcast_to(row, (ncol, ncol)), 0.0), axis=1, keepdims=True)

    @pl.when(j == 0)
    def _():
        m_ref[...] = jnp.full_like(m_ref, -jnp.inf)
        l_ref[...] = jnp.zeros_like(l_ref)
        acc_ref[...] = jnp.zeros_like(acc_ref)
        cb_ref[:, KV_LORA:] = jnp.zeros((cb_ref.shape[0], LANES), BF16)
        qf_ref[...] = q_ref[...].reshape(ncol, QA_W).astype(BF16)

    def update(cb, rr, mask):
        c = cb[:, :KV_LORA]
        ss = jnp.zeros((cb.shape[0], ncol), F32)
        step = 512
        for n0 in range(0, H_M * NOPE_DIM, step):
            kn = _dot(c, wkb_ref[:, n0:n0 + step])
            ss = ss + _dot((kn * kn).astype(BF16), bd_ref[n0:n0 + step, :])
        s = lax.dot_general(cb, qf_ref[...], _NT, preferred_element_type=F32)
        s = s * lax.rsqrt((ss + rr) * (1.0 / QK_DIM) + EPS)
        if mask is not None:
            s = jnp.where(mask, s, -jnp.inf)
        m_old = m_ref[...]
        m_new = jnp.maximum(m_old, jnp.max(s, axis=0, keepdims=True))
        alpha = jnp.exp(m_old - m_new)
        p = jnp.exp(s - m_new)
        l_ref[...] = alpha * l_ref[...] + jnp.sum(p, axis=0, keepdims=True)
        m_ref[...] = m_new
        pv = lax.dot_general(p.astype(BF16), c, _TN, preferred_element_type=F32)
        acc_ref[...] = to_col(alpha) * acc_ref[...] + pv

    rrs = []
    for i in range(npb):
        rows = slice(i * PAGE_SIZE, (i + 1) * PAGE_SIZE)
        cb_ref[rows, :KV_LORA] = c_refs[i][...].astype(BF16)
        r = r_refs[i][...]
        cb_ref[rows, KV_LORA:KV_LORA + ROPE_DIM] = r.astype(BF16)
        rrs.append(jnp.sum(r * r, axis=-1, keepdims=True))
    update(cb_ref[...], jnp.concatenate(rrs, axis=0), None)

    @pl.when(j == pl.num_programs(1) - 1)
    def _():
        rn = rn_ref[...]
        new_ref[...] = jnp.zeros_like(new_ref)
        new_ref[0:nq, :KV_LORA] = cn_ref[...]
        new_ref[0:nq, KV_LORA:KV_LORA + ROPE_DIM] = rn
        pad = new_ref.shape[0] - nq
        rr = jnp.concatenate([jnp.sum(rn * rn, axis=-1, keepdims=True), jnp.zeros((pad, 1), F32)], axis=0)
        t = lax.broadcasted_iota(jnp.int32, (new_ref.shape[0], ncol), 0)
        qq = lax.broadcasted_iota(jnp.int32, (new_ref.shape[0], ncol), 1) % nq
        update(new_ref[...].astype(BF16), rr, t <= qq)
        o_ref[...] = acc_ref[...] * (1.0 / to_col(l_ref[...]))


def _attn_s_call(page_table, qa, ckv_new, kr_new, cache_ckv, cache_krope, wkb, bd):
    n_seq, n_pages = page_table.shape
    nq = qa.shape[2]
    assert nq == 8
    npb = min(8, n_pages)
    assert n_pages % npb == 0
    ncol = H_M * nq

    def page(i):
        return lambda b, j, pt: (0, pt[b * n_pages + j * npb + i], 0, 0)

    c_specs = [pl.BlockSpec((None, None, PAGE_SIZE, KV_LORA), page(i)) for i in range(npb)]
    r_specs = [pl.BlockSpec((None, None, PAGE_SIZE, ROPE_DIM), page(i)) for i in range(npb)]
    grid_spec = pltpu.PrefetchScalarGridSpec(
        num_scalar_prefetch=1,
        grid=(n_seq, n_pages // npb),
        in_specs=[
            pl.BlockSpec((None, H_M, nq, QA_W), lambda b, j, pt: (b, 0, 0, 0)),
            pl.BlockSpec((None, nq, KV_LORA), lambda b, j, pt: (b, 0, 0)),
            pl.BlockSpec((None, nq, ROPE_DIM), lambda b, j, pt: (b, 0, 0)),
            pl.BlockSpec((KV_LORA, H_M * NOPE_DIM), lambda b, j, pt: (0, 0)),
            pl.BlockSpec((H_M * NOPE_DIM, ncol), lambda b, j, pt: (0, 0)),
        ] + c_specs + r_specs,
        out_specs=pl.BlockSpec((None, ncol, KV_LORA), lambda b, j, pt: (b, 0, 0)),
        scratch_shapes=[
            pltpu.VMEM((npb * PAGE_SIZE, QA_W), BF16),
            pltpu.VMEM((2 * nq, QA_W), F32),
            pltpu.VMEM((ncol, QA_W), BF16),
            pltpu.VMEM((1, ncol), F32),
            pltpu.VMEM((1, ncol), F32),
            pltpu.VMEM((ncol, KV_LORA), F32),
        ],
    )
    return pl.pallas_call(
        functools.partial(_attn_s_kernel, npb=npb),
        grid_spec=grid_spec,
        out_shape=jax.ShapeDtypeStruct((n_seq, ncol, KV_LORA), F32),
        compiler_params=_params(("arbitrary", "arbitrary"), 48),
        name="attn_s",
    )(page_table.reshape(-1), qa, ckv_new, kr_new, wkb, bd,
      *([cache_ckv] * npb), *([cache_krope] * npb))


def _vexp_kernel(ctx_ref, wv_ref, o_ref):
    nb, nq, _ = ctx_ref.shape
    o_ref[...] = _dot(ctx_ref[...].reshape(nb * nq, KV_LORA).astype(BF16), wv_ref[...])


def _vexp_call(ctx, wvb):
    n_seq, _, nq, _ = ctx.shape
    return pl.pallas_call(
        _vexp_kernel,
        grid=(H_M,),
        in_specs=[
            pl.BlockSpec((n_seq, None, nq, KV_LORA), lambda h: (0, h, 0, 0)),
            pl.BlockSpec((None, KV_LORA, V_DIM), lambda h: (h, 0, 0)),
        ],
        out_specs=pl.BlockSpec((n_seq * nq, V_DIM), lambda h: (0, h)),
        out_shape=jax.ShapeDtypeStruct((n_seq * nq, D_M), F32),
        compiler_params=_params(("arbitrary",), 32),
        name="vexp",
    )(ctx, wvb)


def _mixout_kernel(a_ref, m_ref, ga_ref, gm_ref, x_ref, gt_ref, w_ref, o_ref, lhs_ref):
    @pl.when(pl.program_id(1) == 0)
    def _():
        def body(i, carry):
            rows = pl.ds(pl.multiple_of(i * ROW_STEP, ROW_STEP), ROW_STEP)
            a = a_ref[rows, :]
            lhs_ref[rows, :D_A] = (a * _rms_scale(a, D_A) * ga_ref[...]).astype(BF16)
            m = m_ref[rows, :]
            lhs_ref[rows, D_A:] = (m * _rms_scale(m, D_M) * gm_ref[...]).astype(BF16)
            return carry

        lax.fori_loop(0, a_ref.shape[0] // ROW_STEP, body, 0)

    o_ref[...] = x_ref[...] + gt_ref[...] * _dot(lhs_ref[...], w_ref[...])


def _mixout_call(tok, a, m, g_a, g_m, x, w_out):
    tm = tok.tile(512)
    tn = 512
    mod_blk = tok.mod_spec(tm, 5)
    if tok.sample:
        gt_spec = pl.BlockSpec((tm, tn), lambda i, j: (i, 5 * (D_MODEL // tn) + j))
    else:
        per_seq = tok.seq_len // tm
        gt_spec = pl.BlockSpec((None, 1, tn), lambda i, j: ((i // per_seq) * N_MOD + 5, 0, j))
    del mod_blk
    return pl.pallas_call(
        _mixout_kernel,
        grid=(tok.n_tok // tm, D_MODEL // tn),
        in_specs=[
            pl.BlockSpec((tm, D_A), lambda i, j: (i, 0)),
            pl.BlockSpec((tm, D_M), lambda i, j: (i, 0)),
            pl.BlockSpec((1, D_A), lambda i, j: (0, 0)),
            pl.BlockSpec((1, D_M), lambda i, j: (0, 0)),
            pl.BlockSpec((tm, tn), lambda i, j: (i, j)),
            gt_spec,
            pl.BlockSpec((D_A + D_M, tn), lambda i, j: (0, j)),
        ],
        out_specs=pl.BlockSpec((tm, tn), lambda i, j: (i, j)),
        out_shape=jax.ShapeDtypeStruct((tok.n_tok, D_MODEL), F32),
        scratch_shapes=[pltpu.VMEM((tm, D_A + D_M), BF16)],
        compiler_params=_params(("arbitrary", "arbitrary"), 48),
        name="mixout",
    )(a, m, g_a, g_m, x, tok.mod, w_out)


def _rope_table(pos):
    inv_freq = ROPE_THETA ** (-jnp.arange(HALF_ROPE, dtype=F32) / HALF_ROPE)
    ang = pos.astype(F32)[:, None] * inv_freq[None, :]
    cos, sin = jnp.cos(ang), jnp.sin(ang)
    return jnp.concatenate([cos, cos, sin, sin], axis=-1)


def _dup_rope_gain(g):
    g1, g2 = g[NOPE_DIM:NOPE_DIM + HALF_ROPE], g[NOPE_DIM + HALF_ROPE:]
    return jnp.concatenate([g1, g2, g1, g2])[None, :]


def _layer(tok, x, pos, p, attn_fn):
    x = _ffn_call(tok, x, 0, p["g_norm1"], p["wg1"], p["wu1"], p["wd1"])
    uv = _uv_call(tok, x, p["g_norm2"], p["wu"], p["wv"], p["g_v_a"],
                  p["ws_s"] if tok.sample else p["ws_p"], p["bs_s"] if tok.sample else p["bs_p"])
    a_out = uv[0]
    tk = _rope_table(pos)
    cq, ckv, krd = _qkr_call(tok, x, p["g_norm2"], p["wb"], p["g_q_a"], p["g_kv_a"], tk)
    m_out = attn_fn(cq, ckv, krd, tk)
    x = _mixout_call(tok, a_out, m_out, p["g_out_a"], p["g_out_m"], x, p["w_out"])
    x = _ffn_call(tok, x, 6, p["g_norm3"], p["wg2"], p["wu2"], p["wd2"])
    return x, ckv, krd[:, :ROPE_DIM], (uv[1] if tok.sample else None)


def kernel(x_prompt, x_sample, cache_ckv, cache_krope, page_table, c_prompt, c_sample, w_ada, b_ada, g_norm1, w_ff1_gate, w_ff1_up, w_ff1_down, g_norm2, w_in, w_s, b_s, g_v_a, g_q_a, w_q_b, g_kv_a, w_kv_b, g_q_norm, g_k_norm, g_out_a, g_out_m, w_out, g_norm3, w_ff2_gate, w_ff2_up, w_ff2_down):
    n_seq_p, seq_len, _ = x_prompt.shape
    n_seq_s, dec_len, _ = x_sample.shape
    n_pages = page_table.shape[1]
    past = n_pages * PAGE_SIZE
    depth = w_ada.shape[0]
    assert depth == 1 and dec_len == 8

    w_in0 = w_in[0]
    o = 2 * D_A
    w_kr = w_in0[:, o + Q_LORA + KV_LORA:]
    w1, w2 = w_kr[:, :HALF_ROPE], w_kr[:, HALF_ROPE:]
    wq = w_q_b[0]
    qn, q1, q2 = wq[..., :NOPE_DIM], wq[..., NOPE_DIM:NOPE_DIM + HALF_ROPE], wq[..., NOPE_DIM + HALF_ROPE:]
    wkv = w_kv_b[0]
    wk_n, wv_n = wkv[..., :NOPE_DIM], wkv[..., NOPE_DIM:]
    gq, gk = g_q_norm[0], g_k_norm[0]
    tril = jnp.tril(jnp.ones((CHUNK, CHUNK), dtype=bool))
    ws_t = jnp.where(tril[None], w_s[0], 0.0)
    p = {
        "g_norm1": g_norm1, "g_norm2": g_norm2, "g_norm3": g_norm3,
        "wg1": w_ff1_gate[0].astype(BF16), "wu1": w_ff1_up[0].astype(BF16), "wd1": w_ff1_down[0].astype(BF16),
        "wg2": w_ff2_gate[0].astype(BF16), "wu2": w_ff2_up[0].astype(BF16), "wd2": w_ff2_down[0].astype(BF16),
        "wu": w_in0[:, :D_A].astype(BF16), "wv": w_in0[:, D_A:o].astype(BF16),
        "wb": jnp.concatenate([w_in0[:, o:o + Q_LORA + KV_LORA], w1, w2, -w2, w1], axis=1).astype(BF16),
        "g_v_a": g_v_a, "g_q_a": g_q_a, "g_kv_a": g_kv_a,
        "ws_p": ws_t.astype(BF16),
        "bs_p": jnp.broadcast_to(b_s[0][:, :, None], (H_A, CHUNK, HEAD_A)),
        "ws_s": jnp.repeat(ws_t[:, :dec_len, :dec_len].transpose(2, 1, 0), HEAD_A, axis=2),
        "bs_s": jnp.repeat(b_s[0][:, :dec_len].T, HEAD_A, axis=1),
        "g_out_a": g_out_a, "g_out_m": g_out_m, "w_out": w_out[0].astype(BF16),
    }
    wqb = jnp.concatenate([qn, q1, q2, -q2, q1], axis=-1).reshape(Q_LORA, H_M * QH_W).astype(BF16)
    wkvb = jnp.concatenate([wk_n.reshape(KV_LORA, -1), wv_n.reshape(KV_LORA, -1)], axis=1).astype(BF16)
    wkb = wk_n.reshape(KV_LORA, H_M * NOPE_DIM).astype(BF16)
    wkbt = wk_n.transpose(1, 2, 0).astype(BF16)
    wvb = wv_n.transpose(1, 0, 2).astype(BF16)
    gq_dup, gk_dup = _dup_rope_gain(gq), _dup_rope_gain(gk)
    gq_n, gk_n = gq[None, :NOPE_DIM], gk[None, :NOPE_DIM]

    n_c = n_seq_p + n_seq_s
    pad = (-n_c) % 8
    c_all = jnp.concatenate([c_prompt, c_sample, jnp.zeros((pad, D_MODEL), F32)], axis=0)
    mod = _mod_call(c_all, w_ada[0], b_ada)
    mod_p = mod[:n_seq_p].reshape(n_seq_p * N_MOD, 1, D_MODEL)
    mod_s = jnp.repeat(mod[n_seq_p:n_c], dec_len, axis=0)

    tok_p = _Tokens(False, n_seq_p * seq_len, seq_len, mod_p)

    def attn_prompt(cq, ckv, krd, tk):
        q, k, v = _qpost_call(tok_p, cq, ckv, krd, tk * gq_dup, gq_n, gk_n, gk_dup, wqb, wkvb, None)
        return _attn_p_call(q, k, v, n_seq_p, seq_len)

    yp, ckv_p, kr_p, _ = _layer(tok_p, x_prompt.reshape(-1, D_MODEL), jnp.arange(seq_len), p, attn_prompt)

    tok_s = _Tokens(True, n_seq_s * dec_len, dec_len, mod_s)
    ncol = H_M * dec_len
    col_head = jnp.arange(ncol) // dec_len
    bd = (jnp.arange(H_M * NOPE_DIM)[:, None] // NOPE_DIM == col_head[None, :]).astype(BF16)

    def attn_sample(cq, ckv, krd, tk):
        qa = _qpost_call(tok_s, cq, None, None, tk * (gq_dup * gk_dup), gq_n * gk_n, None, None, wqb, None, wkbt)
        ctx = _attn_s_call(page_table, qa, ckv.reshape(n_seq_s, dec_len, KV_LORA),
                           krd[:, :ROPE_DIM].reshape(n_seq_s, dec_len, ROPE_DIM),
                           cache_ckv, cache_krope, wkb, bd)
        return _vexp_call(ctx.reshape(n_seq_s, H_M, dec_len, KV_LORA), wvb)

    pos_s = jnp.tile(past + jnp.arange(dec_len), n_seq_s)
    ys, ckv_s, kr_s, va_s = _layer(tok_s, x_sample.reshape(-1, D_MODEL), pos_s, p, attn_sample)

    return (
        yp.reshape(n_seq_p, seq_len, D_MODEL),
        ys.reshape(n_seq_s, dec_len, D_MODEL),
        ckv_p.reshape(1, n_seq_p, seq_len, KV_LORA),
        kr_p.reshape(1, n_seq_p, seq_len, ROPE_DIM),
        ckv_s.reshape(1, n_seq_s, dec_len, KV_LORA),
        kr_s.reshape(1, n_seq_s, dec_len, ROPE_DIM),
        va_s.reshape(1, n_seq_s, dec_len, H_A, HEAD_A),
    )
```

```python
import functools

import jax
import jax.numpy as jnp
from jax import lax
from jax.experimental import pallas as pl
from jax.experimental.pallas import tpu as pltpu

F32 = jnp.float32
BF16 = jnp.bfloat16

D_MODEL = 4096
D_A = 2048
HEAD_A = 128
H_A = 16
CHUNK = 128
D_M = 2048
V_DIM = 128
H_M = 16
NOPE_DIM = 128
ROPE_DIM = 64
HALF_ROPE = ROPE_DIM // 2
QK_DIM = NOPE_DIM + ROPE_DIM
Q_LORA = 1024
KV_LORA = 512
ROPE_THETA = 10000.0
D_FF = 11008
N_MOD = 9
PAGE_SIZE = 128
EPS = 1e-6

QH_W = 256
QA_W = KV_LORA + 128
LANES = 128
MIB = 1024 * 1024
ROW_STEP = 16
SEQ_ROWS = 8
KEY_CHUNK = 256
ATTN_S_PAGES = 16
FFN_DOWN_COLS = 512

_NT = (((1,), (1,)), ((), ()))
_TN = (((0,), (0,)), ((), ()))


def _dot(a, b):
    return jnp.dot(a, b, preferred_element_type=F32)


def _params(semantics, vmem_mib):
    return pltpu.CompilerParams(dimension_semantics=semantics, vmem_limit_bytes=vmem_mib * MIB)


def _single(block_shape, index_map):
    return pl.BlockSpec(block_shape, index_map, pipeline_mode=pl.Buffered(1))


def _ada_norm(x, g, sh, sc):
    r = lax.rsqrt(jnp.mean(x * x, axis=-1, keepdims=True) + EPS)
    return (x * r * g) * (1.0 + sc) + sh


def _mod_rows(ref, i):
    if ref.shape[0] == 1:
        return ref[...]
    per = ROW_STEP // SEQ_ROWS
    parts = [jnp.broadcast_to(ref[pl.ds(i * per + k, 1), :], (SEQ_ROWS, ref.shape[1])) for k in range(per)]
    return jnp.concatenate(parts, axis=0)


def _row_loop(n_rows, body):
    def step(i, carry):
        body(i, pl.ds(pl.multiple_of(i * ROW_STEP, ROW_STEP), ROW_STEP))
        return carry

    lax.fori_loop(0, n_rows // ROW_STEP, step, 0)


def _ada_norm_rows(h_ref, x_ref, g_ref, sh_ref, sc_ref):
    def body(i, rows):
        h_ref[rows, :] = _ada_norm(x_ref[rows, :], g_ref[...], _mod_rows(sh_ref, i), _mod_rows(sc_ref, i)).astype(BF16)

    _row_loop(x_ref.shape[0], body)


def _rms_scale(x, width):
    return lax.rsqrt(jnp.sum(x * x, axis=-1, keepdims=True) * (1.0 / width) + EPS)


def _gelu(x):
    return jax.nn.gelu(x)


class _Tokens:
    def __init__(self, sample, n_tok, seq_len, mod):
        self.sample = sample
        self.n_tok = n_tok
        self.seq_len = seq_len
        self.mod = mod

    def mod_spec(self, tm, k, width=D_MODEL, col=None):
        per_row = D_MODEL // width
        col = col or (lambda *_: 0)
        if self.sample:
            return pl.BlockSpec((tm // SEQ_ROWS, width), lambda i, *r: (i, k * per_row + col(i, *r)))
        per_seq = self.seq_len // tm
        return pl.BlockSpec((None, 1, width), lambda i, *r: ((i // per_seq) * N_MOD + k, 0, col(i, *r)))

    def pos_spec(self, tm, width):
        if self.sample:
            return pl.BlockSpec((tm, width), lambda i, *_: (i, 0))
        per_seq = self.seq_len // tm
        return pl.BlockSpec((tm, width), lambda i, *_: (i % per_seq, 0))

    def tile(self, want):
        tm = min(want, self.n_tok if self.sample else self.seq_len)
        assert self.n_tok % tm == 0 and (self.sample or self.seq_len % tm == 0)
        return tm


def _mod_kernel(c_ref, w_ref, b_ref, o_ref):
    c = c_ref[...]
    a = (c * jax.nn.sigmoid(c)).astype(BF16)
    o_ref[...] = _dot(a, w_ref[...].astype(BF16)) + b_ref[...]


def _mod_call(c, w_ada, b_ada):
    m, n = c.shape[0], w_ada.shape[1]
    tn = 1024
    return pl.pallas_call(
        _mod_kernel,
        grid=(n // tn,),
        in_specs=[
            pl.BlockSpec((m, D_MODEL), lambda j: (0, 0)),
            pl.BlockSpec((D_MODEL, tn), lambda j: (0, j)),
            pl.BlockSpec((1, tn), lambda j: (0, j)),
        ],
        out_specs=pl.BlockSpec((m, tn), lambda j: (0, j)),
        out_shape=jax.ShapeDtypeStruct((m, n), F32),
        compiler_params=_params(("arbitrary",), 56),
        name="mod",
    )(c, w_ada, b_ada)


def _ffn_kernel(x_ref, sh_ref, sc_ref, gt_ref, g_ref, wg_ref, wu_ref, wd_ref, o_ref, h_ref):
    k = pl.program_id(1)

    @pl.when(k == 0)
    def _():
        _ada_norm_rows(h_ref, x_ref, g_ref, sh_ref, sc_ref)
        o_ref[...] = jnp.zeros_like(o_ref)

    h = h_ref[...]
    g = _dot(h, wg_ref[...])
    u = _dot(h, wu_ref[...])
    a = (g * jax.nn.sigmoid(g) * u).astype(BF16)
    for n0 in range(0, D_MODEL, FFN_DOWN_COLS):
        cols = slice(n0, n0 + FFN_DOWN_COLS)
        o_ref[:, cols] += _dot(a, wd_ref[:, cols])

    @pl.when(k == pl.num_programs(1) - 1)
    def _():
        def body(i, rows):
            o_ref[rows, :] = x_ref[rows, :] + (0.5 * _mod_rows(gt_ref, i)) * o_ref[rows, :]

        _row_loop(o_ref.shape[0], body)


def _ffn_call(tok, x, mod_base, g_norm, wg, wu, wd):
    tm = tok.tile(512 if tok.sample else 1024)
    tf = 256
    n_ff = wg.shape[1]
    return pl.pallas_call(
        _ffn_kernel,
        grid=(tok.n_tok // tm, n_ff // tf),
        in_specs=[
            _single((tm, D_MODEL), lambda i, k: (i, 0)),
            tok.mod_spec(tm, mod_base),
            tok.mod_spec(tm, mod_base + 1),
            tok.mod_spec(tm, mod_base + 2),
            pl.BlockSpec((1, D_MODEL), lambda i, k: (0, 0)),
            pl.BlockSpec((D_MODEL, tf), lambda i, k: (0, k)),
            pl.BlockSpec((D_MODEL, tf), lambda i, k: (0, k)),
            pl.BlockSpec((tf, D_MODEL), lambda i, k: (k, 0)),
        ],
        out_specs=_single((tm, D_MODEL), lambda i, k: (i, 0)),
        out_shape=jax.ShapeDtypeStruct((tok.n_tok, D_MODEL), F32),
        scratch_shapes=[pltpu.VMEM((tm, D_MODEL), BF16)],
        compiler_params=_params(("arbitrary", "arbitrary"), 60),
        name="ffn",
    )(x, tok.mod, tok.mod, tok.mod, g_norm, wg, wu, wd)


def _uv_kernel(x_ref, sh_ref, sc_ref, g_ref, wu_ref, wv_ref, gv_ref, ws_ref, bs_ref, *rest, sample):
    if sample:
        a_ref, va_ref, h_ref = rest
    else:
        a_ref, h_ref = rest
    j = pl.program_id(1)

    @pl.when(j == 0)
    def _():
        _ada_norm_rows(h_ref, x_ref, g_ref, sh_ref, sc_ref)

    h = h_ref[...]
    tm = h.shape[0]
    u = _gelu(_dot(h, wu_ref[...]))
    v = _gelu(_dot(h, wv_ref[...]))
    for hh in range(2):
        cols = slice(hh * HEAD_A, (hh + 1) * HEAD_A)
        vh = v[:, cols]
        vh = vh * _rms_scale(vh, HEAD_A) * gv_ref[...]
        if sample:
            va_ref[:, cols] = vh
            nb = tm // 8
            v3 = vh.reshape(nb, 8, HEAD_A)
            mixed = jnp.broadcast_to(bs_ref[:, cols][None], (nb, 8, HEAD_A))
            for s in range(8):
                row = jnp.broadcast_to(v3[:, s:s + 1, :], (nb, 8, HEAD_A))
                mixed = mixed + row * ws_ref[s, :, cols][None]
            a_ref[:, cols] = u[:, cols] * mixed.reshape(tm, HEAD_A)
        else:
            w = ws_ref[hh]
            vb = vh.astype(BF16)
            for c in range(tm // CHUNK):
                rows = slice(c * CHUNK, (c + 1) * CHUNK)
                mixed = _dot(w, vb[rows]) + bs_ref[hh]
                a_ref[rows, cols] = u[rows, cols] * mixed


def _uv_call(tok, x, g_norm, wu, wv, g_v, ws, bs):
    sample = tok.sample
    tm = tok.tile(512 if sample else 1024)
    if sample:
        ws_spec = pl.BlockSpec((8, 8, 2 * HEAD_A), lambda i, j: (0, 0, j))
        bs_spec = pl.BlockSpec((8, 2 * HEAD_A), lambda i, j: (0, j))
    else:
        assert tm % CHUNK == 0
        ws_spec = pl.BlockSpec((2, CHUNK, CHUNK), lambda i, j: (j, 0, 0))
        bs_spec = pl.BlockSpec((2, CHUNK, HEAD_A), lambda i, j: (j, 0, 0))
    blk = pl.BlockSpec((tm, 2 * HEAD_A), lambda i, j: (i, j))
    out_shape = [jax.ShapeDtypeStruct((tok.n_tok, D_A), F32)]
    out_specs = [blk]
    if sample:
        out_shape.append(jax.ShapeDtypeStruct((tok.n_tok, D_A), F32))
        out_specs.append(blk)
    return pl.pallas_call(
        functools.partial(_uv_kernel, sample=sample),
        grid=(tok.n_tok // tm, H_A // 2),
        in_specs=[
            _single((tm, D_MODEL), lambda i, j: (i, 0)),
            tok.mod_spec(tm, 3),
            tok.mod_spec(tm, 4),
            pl.BlockSpec((1, D_MODEL), lambda i, j: (0, 0)),
            pl.BlockSpec((D_MODEL, 2 * HEAD_A), lambda i, j: (0, j)),
            pl.BlockSpec((D_MODEL, 2 * HEAD_A), lambda i, j: (0, j)),
            pl.BlockSpec((1, HEAD_A), lambda i, j: (0, 0)),
            ws_spec,
            bs_spec,
        ],
        out_specs=out_specs,
        out_shape=out_shape,
        scratch_shapes=[pltpu.VMEM((tm, D_MODEL), BF16)],
        compiler_params=_params(("arbitrary", "arbitrary"), 48),
        name="uv",
    )(x, tok.mod, tok.mod, g_norm, wu, wv, g_v, ws, bs)


def _qkr_kernel(x_ref, sh_ref, sc_ref, g_ref, wb_ref, gq_ref, gkv_ref, tk_ref, cq_ref, ckv_ref, kr_ref,
                h_ref):
    _ada_norm_rows(h_ref, x_ref, g_ref, sh_ref, sc_ref)
    p = _dot(h_ref[...], wb_ref[...])
    cq = p[:, :Q_LORA]
    cq_ref[...] = (cq * _rms_scale(cq, Q_LORA) * gq_ref[...]).astype(BF16)
    ckv = p[:, Q_LORA:Q_LORA + KV_LORA]
    ckv_ref[...] = ckv * _rms_scale(ckv, KV_LORA) * gkv_ref[...]
    e = p[:, Q_LORA + KV_LORA:] * tk_ref[...]
    kr_ref[...] = e + pltpu.roll(e, 2 * HALF_ROPE, axis=1)


def _qkr_call(tok, x, g_norm, wb, g_q_a, g_kv_a, tk):
    tm = tok.tile(512)
    nb = wb.shape[1]
    row = lambda w: pl.BlockSpec((tm, w), lambda i: (i, 0))
    return pl.pallas_call(
        _qkr_kernel,
        grid=(tok.n_tok // tm,),
        in_specs=[
            row(D_MODEL),
            tok.mod_spec(tm, 3),
            tok.mod_spec(tm, 4),
            pl.BlockSpec((1, D_MODEL), lambda i: (0, 0)),
            _single((D_MODEL, nb), lambda i: (0, 0)),
            pl.BlockSpec((1, Q_LORA), lambda i: (0, 0)),
            pl.BlockSpec((1, KV_LORA), lambda i: (0, 0)),
            tok.pos_spec(tm, LANES),
        ],
        out_specs=[row(Q_LORA), row(KV_LORA), row(LANES)],
        out_shape=[
            jax.ShapeDtypeStruct((tok.n_tok, Q_LORA), BF16),
            jax.ShapeDtypeStruct((tok.n_tok, KV_LORA), F32),
            jax.ShapeDtypeStruct((tok.n_tok, LANES), F32),
        ],
        scratch_shapes=[pltpu.VMEM((tm, D_MODEL), BF16)],
        compiler_params=_params(("arbitrary",), 56),
        name="qkr",
    )(x, tok.mod, tok.mod, g_norm, wb, g_q_a, g_kv_a, tk)


def _q_heads(cq_ref, wqb_ref, tq_ref, ga_ref):
    q = _dot(cq_ref[...], wqb_ref[...])
    tq = tq_ref[...]
    ga = ga_ref[...]
    for h in range(H_M):
        a = q[:, h * QH_W:h * QH_W + NOPE_DIM]
        b = q[:, h * QH_W + NOPE_DIM:(h + 1) * QH_W]
        ss = jnp.sum(a * a + 0.5 * (b * b), axis=-1, keepdims=True)
        r = lax.rsqrt(ss * (1.0 / QK_DIM) + EPS) * (QK_DIM ** -0.5)
        yield h, a * ga * r, b * tq * r


def _qpost_p_kernel(cq_ref, ckv_ref, kr_ref, tq_ref, ga_ref, gkn_ref, gkr_ref, wqb_ref, wkvb_ref,
                    q_ref, k_ref, v_ref):
    for h, a, b in _q_heads(cq_ref, wqb_ref, tq_ref, ga_ref):
        q_ref[:, h * QH_W:h * QH_W + NOPE_DIM] = a.astype(BF16)
        q_ref[:, h * QH_W + NOPE_DIM:(h + 1) * QH_W] = b.astype(BF16)
    kv = _dot(ckv_ref[...].astype(BF16), wkvb_ref[...])
    v_ref[...] = kv[:, D_M:].astype(BF16)
    kr = kr_ref[...]
    rr = 0.5 * jnp.sum(kr * kr, axis=-1, keepdims=True)
    krg = kr * gkr_ref[...]
    for h in range(H_M):
        kn = kv[:, h * NOPE_DIM:(h + 1) * NOPE_DIM]
        r = lax.rsqrt((jnp.sum(kn * kn, axis=-1, keepdims=True) + rr) * (1.0 / QK_DIM) + EPS)
        k_ref[:, h * QH_W:h * QH_W + NOPE_DIM] = (kn * gkn_ref[...] * r).astype(BF16)
        k_ref[:, h * QH_W + NOPE_DIM:(h + 1) * QH_W] = (krg * r).astype(BF16)


def _qpost_s_kernel(cq_ref, tq_ref, ga_ref, wqb_ref, wkbt_ref, qa_ref):
    tm = cq_ref.shape[0]
    lane = lax.broadcasted_iota(jnp.int32, (tm, LANES), 1)
    for h, a, b in _q_heads(cq_ref, wqb_ref, tq_ref, ga_ref):
        qabs = _dot(a.astype(BF16), wkbt_ref[h])
        qr = jnp.where(lane < ROPE_DIM, b + pltpu.roll(b, ROPE_DIM, axis=1), 0.0)
        qa_ref[:, h, :, :KV_LORA] = qabs.reshape(tm // 8, 8, KV_LORA)
        qa_ref[:, h, :, KV_LORA:] = qr.reshape(tm // 8, 8, LANES)


def _qpost_call(tok, cq, ckv, kr, tq, ga, gkn, gkr, wqb, wkvb, wkbt):
    tm = tok.tile(256)
    row = lambda w: pl.BlockSpec((tm, w), lambda i: (i, 0))
    one = lambda w: pl.BlockSpec((1, w), lambda i: (0, 0))
    n = tok.n_tok
    if tok.sample:
        return pl.pallas_call(
            _qpost_s_kernel,
            grid=(n // tm,),
            in_specs=[
                row(Q_LORA), tok.pos_spec(tm, LANES), one(NOPE_DIM),
                _single((Q_LORA, H_M * QH_W), lambda i: (0, 0)),
                _single((H_M, NOPE_DIM, KV_LORA), lambda i: (0, 0, 0)),
            ],
            out_specs=pl.BlockSpec((tm // 8, H_M, 8, QA_W), lambda i: (i, 0, 0, 0)),
            out_shape=jax.ShapeDtypeStruct((n // 8, H_M, 8, QA_W), F32),
            compiler_params=_params(("arbitrary",), 56),
            name="qpost_s",
        )(cq, tq, ga, wqb, wkbt)
    return pl.pallas_call(
        _qpost_p_kernel,
        grid=(n // tm,),
        in_specs=[
            row(Q_LORA), row(KV_LORA), row(LANES), tok.pos_spec(tm, LANES),
            one(NOPE_DIM), one(NOPE_DIM), one(LANES),
            _single((Q_LORA, H_M * QH_W), lambda i: (0, 0)),
            _single((KV_LORA, 2 * D_M), lambda i: (0, 0)),
        ],
        out_specs=[row(H_M * QH_W), row(H_M * QH_W), row(D_M)],
        out_shape=[
            jax.ShapeDtypeStruct((n, H_M * QH_W), BF16),
            jax.ShapeDtypeStruct((n, H_M * QH_W), BF16),
            jax.ShapeDtypeStruct((n, D_M), BF16),
        ],
        compiler_params=_params(("arbitrary",), 56),
        name="qpost_p",
    )(cq, ckv, kr, tq, ga, gkn, gkr, wqb, wkvb)


def _attn_p_kernel(q_ref, k_ref, v_ref, o_ref):
    tq = q_ref.shape[0]
    qi = pl.program_id(2)
    for n in range(k_ref.shape[0] // tq):
        @pl.when(qi == n)
        def _(n=n):
            nk = (n + 1) * tq
            s = lax.dot_general(q_ref[...], k_ref[0:nk, :], _NT, preferred_element_type=F32)
            q_pos = n * tq + lax.broadcasted_iota(jnp.int32, (tq, nk), 0)
            k_pos = lax.broadcasted_iota(jnp.int32, (tq, nk), 1)
            s = jnp.where(k_pos <= q_pos, s, -jnp.inf)
            e = jnp.exp(s - jnp.max(s, axis=-1, keepdims=True))
            p = e * (1.0 / jnp.sum(e, axis=-1, keepdims=True))
            o_ref[...] = _dot(p.astype(BF16), v_ref[0:nk, :])


def _attn_p_call(q, k, v, n_seq, seq_len):
    tq = min(512, seq_len)
    nq = seq_len // tq
    return pl.pallas_call(
        _attn_p_kernel,
        grid=(n_seq, H_M, nq),
        in_specs=[
            pl.BlockSpec((tq, QH_W), lambda b, h, i: (b * nq + i, h)),
            pl.BlockSpec((seq_len, QH_W), lambda b, h, i: (b, h)),
            pl.BlockSpec((seq_len, V_DIM), lambda b, h, i: (b, h)),
        ],
        out_specs=pl.BlockSpec((tq, V_DIM), lambda b, h, i: (b * nq + i, h)),
        out_shape=jax.ShapeDtypeStruct((n_seq * seq_len, D_M), F32),
        compiler_params=_params(("arbitrary", "arbitrary", "arbitrary"), 48),
        name="attn_p",
    )(q, k, v)


def _attn_s_kernel(pt_ref, q_ref, cn_ref, rnt_ref, wkbt_ref, *rest, npb):
    del pt_ref
    c_refs, r_refs = rest[:npb], rest[npb:2 * npb]
    o_ref, lhs_ref, qr_ref, cb_ref, rt_ref, m_ref, l_ref, acc_ref = rest[2 * npb:]
    j = pl.program_id(1)
    nq = q_ref.shape[1]
    ncol = H_M * nq
    nkd = H_M * NOPE_DIM

    @pl.when(j == 0)
    def _():
        m_ref[...] = jnp.full_like(m_ref, -jnp.inf)
        l_ref[...] = jnp.zeros_like(l_ref)
        acc_ref[...] = jnp.zeros_like(acc_ref)
        q = q_ref[...].reshape(ncol, QA_W)
        lhs_ref[0:nkd, :] = wkbt_ref[...]
        lhs_ref[nkd:, :] = q[:, :KV_LORA].astype(BF16)
        qr_ref[...] = q[:, KV_LORA:].astype(BF16)
        rt_ref[ROPE_DIM:, :] = jnp.zeros((rt_ref.shape[0] - ROPE_DIM, rt_ref.shape[1]), BF16)

    def update(mk, stage, mask):
        chunk = min(KEY_CHUNK, mk)
        parts = []
        for n0 in range(0, mk, chunk):
            keys = slice(n0, n0 + chunk)
            rr = stage(n0, chunk)
            r = lax.dot_general(lhs_ref[...], cb_ref[keys, :], _NT, preferred_element_type=F32)
            kn = r[:nkd].reshape(H_M, NOPE_DIM // 8, 8, chunk)
            tot = jnp.sum(jnp.sum(kn * kn, axis=1), axis=1, keepdims=True)
            s3 = (r[nkd:] + _dot(qr_ref[...], rt_ref[:, keys])).reshape(H_M, nq, chunk)
            inv = lax.rsqrt((tot + rr[None]) * (1.0 / QK_DIM) + EPS)
            parts.append((s3 * inv).reshape(ncol, chunk))
        s = parts[0] if len(parts) == 1 else jnp.concatenate(parts, axis=1)
        if mask is not None:
            s = jnp.where(mask, s, -jnp.inf)
        m_old = m_ref[...]
        m_new = jnp.maximum(m_old, jnp.max(s, axis=1, keepdims=True))
        alpha = jnp.exp(m_old - m_new)
        p = jnp.exp(s - m_new)
        l_ref[...] = alpha * l_ref[...] + jnp.sum(p, axis=1, keepdims=True)
        m_ref[...] = m_new
        acc_ref[...] = alpha * acc_ref[...] + _dot(p.astype(BF16), cb_ref[0:mk, :])

    def stage_pages(n0, n):
        rrs = []
        for i in range(n0 // PAGE_SIZE, (n0 + n) // PAGE_SIZE):
            keys = slice(i * PAGE_SIZE, (i + 1) * PAGE_SIZE)
            cb_ref[keys, :] = c_refs[i][...].astype(BF16)
            r = r_refs[i][...]
            rt_ref[0:ROPE_DIM, keys] = r.astype(BF16)
            rrs.append(jnp.sum(r * r, axis=0, keepdims=True))
        return rrs[0] if len(rrs) == 1 else jnp.concatenate(rrs, axis=1)

    update(npb * PAGE_SIZE, stage_pages, None)

    @pl.when(j == pl.num_programs(1) - 1)
    def _():
        def stage_new(n0, n):
            pad = jnp.zeros((n - nq, KV_LORA), F32)
            cb_ref[0:n, :] = jnp.concatenate([cn_ref[...], pad], axis=0).astype(BF16)
            rn = rnt_ref[...]
            rt_ref[0:ROPE_DIM, 0:n] = rn.astype(BF16)
            return jnp.sum(rn * rn, axis=0, keepdims=True)

        t = lax.broadcasted_iota(jnp.int32, (ncol, PAGE_SIZE), 1)
        qq = lax.broadcasted_iota(jnp.int32, (ncol, PAGE_SIZE), 0) % nq
        update(PAGE_SIZE, stage_new, t <= qq)
        o_ref[...] = acc_ref[...] * (1.0 / l_ref[...])


def _attn_s_call(page_table, qa, ckv_new, krt_new, cache_ckv, cache_krope_t, wkbt):
    n_seq, n_pages = page_table.shape
    nq = qa.shape[2]
    assert nq == SEQ_ROWS
    npb = min(ATTN_S_PAGES, n_pages)
    assert n_pages % npb == 0
    ncol = H_M * nq
    nkd = H_M * NOPE_DIM

    def page(i):
        return lambda b, j, pt: (0, pt[b * n_pages + j * npb + i], 0, 0)

    c_specs = [pl.BlockSpec((None, None, PAGE_SIZE, KV_LORA), page(i)) for i in range(npb)]
    r_specs = [pl.BlockSpec((None, None, ROPE_DIM, PAGE_SIZE), page(i)) for i in range(npb)]
    grid_spec = pltpu.PrefetchScalarGridSpec(
        num_scalar_prefetch=1,
        grid=(n_seq, n_pages // npb),
        in_specs=[
            pl.BlockSpec((None, H_M, nq, QA_W), lambda b, j, pt: (b, 0, 0, 0)),
            pl.BlockSpec((None, nq, KV_LORA), lambda b, j, pt: (b, 0, 0)),
            pl.BlockSpec((None, ROPE_DIM, PAGE_SIZE), lambda b, j, pt: (b, 0, 0)),
            pl.BlockSpec((nkd, KV_LORA), lambda b, j, pt: (0, 0)),
        ] + c_specs + r_specs,
        out_specs=pl.BlockSpec((None, ncol, KV_LORA), lambda b, j, pt: (b, 0, 0)),
        scratch_shapes=[
            pltpu.VMEM((nkd + ncol, KV_LORA), BF16),
            pltpu.VMEM((ncol, LANES), BF16),
            pltpu.VMEM((npb * PAGE_SIZE, KV_LORA), BF16),
            pltpu.VMEM((LANES, npb * PAGE_SIZE), BF16),
            pltpu.VMEM((ncol, 1), F32),
            pltpu.VMEM((ncol, 1), F32),
            pltpu.VMEM((ncol, KV_LORA), F32),
        ],
    )
    return pl.pallas_call(
        functools.partial(_attn_s_kernel, npb=npb),
        grid_spec=grid_spec,
        out_shape=jax.ShapeDtypeStruct((n_seq, ncol, KV_LORA), F32),
        compiler_params=_params(("arbitrary", "arbitrary"), 48),
        name="attn_s",
    )(page_table.reshape(-1), qa, ckv_new, krt_new, wkbt,
      *([cache_ckv] * npb), *([cache_krope_t] * npb))


def _vexp_kernel(ctx_ref, wv_ref, o_ref):
    nb, nq, _ = ctx_ref.shape
    o_ref[...] = _dot(ctx_ref[...].reshape(nb * nq, KV_LORA).astype(BF16), wv_ref[...])


def _vexp_call(ctx, wvb):
    n_seq, _, nq, _ = ctx.shape
    return pl.pallas_call(
        _vexp_kernel,
        grid=(H_M,),
        in_specs=[
            pl.BlockSpec((n_seq, None, nq, KV_LORA), lambda h: (0, h, 0, 0)),
            pl.BlockSpec((None, KV_LORA, V_DIM), lambda h: (h, 0, 0)),
        ],
        out_specs=pl.BlockSpec((n_seq * nq, V_DIM), lambda h: (0, h)),
        out_shape=jax.ShapeDtypeStruct((n_seq * nq, D_M), F32),
        compiler_params=_params(("arbitrary",), 32),
        name="vexp",
    )(ctx, wvb)


def _mixout_kernel(a_ref, m_ref, ga_ref, gm_ref, x_ref, gt_ref, w_ref, o_ref, lhs_ref):
    @pl.when(pl.program_id(1) == 0)
    def _():
        def body(i, rows):
            a = a_ref[rows, :]
            lhs_ref[rows, :D_A] = (a * _rms_scale(a, D_A) * ga_ref[...]).astype(BF16)
            m = m_ref[rows, :]
            lhs_ref[rows, D_A:] = (m * _rms_scale(m, D_M) * gm_ref[...]).astype(BF16)

        _row_loop(a_ref.shape[0], body)

    y = _dot(lhs_ref[...], w_ref[...])
    if gt_ref.shape[0] == 1:
        o_ref[...] = x_ref[...] + gt_ref[...] * y
    else:
        o_ref[...] = y

        def body(i, rows):
            o_ref[rows, :] = x_ref[rows, :] + _mod_rows(gt_ref, i) * o_ref[rows, :]

        _row_loop(o_ref.shape[0], body)


def _mixout_call(tok, a, m, g_a, g_m, x, w_out):
    tm = tok.tile(512)
    tn = 512
    gt_spec = tok.mod_spec(tm, 5, width=tn, col=lambda i, j: j)
    return pl.pallas_call(
        _mixout_kernel,
        grid=(tok.n_tok // tm, D_MODEL // tn),
        in_specs=[
            pl.BlockSpec((tm, D_A), lambda i, j: (i, 0)),
            pl.BlockSpec((tm, D_M), lambda i, j: (i, 0)),
            pl.BlockSpec((1, D_A), lambda i, j: (0, 0)),
            pl.BlockSpec((1, D_M), lambda i, j: (0, 0)),
            pl.BlockSpec((tm, tn), lambda i, j: (i, j)),
            gt_spec,
            pl.BlockSpec((D_A + D_M, tn), lambda i, j: (0, j)),
        ],
        out_specs=pl.BlockSpec((tm, tn), lambda i, j: (i, j)),
        out_shape=jax.ShapeDtypeStruct((tok.n_tok, D_MODEL), F32),
        scratch_shapes=[pltpu.VMEM((tm, D_A + D_M), BF16)],
        compiler_params=_params(("arbitrary", "arbitrary"), 48),
        name="mixout",
    )(a, m, g_a, g_m, x, tok.mod, w_out)


def _rope_table(pos):
    inv_freq = ROPE_THETA ** (-jnp.arange(HALF_ROPE, dtype=F32) / HALF_ROPE)
    ang = pos.astype(F32)[:, None] * inv_freq[None, :]
    cos, sin = jnp.cos(ang), jnp.sin(ang)
    return jnp.concatenate([cos, cos, sin, sin], axis=-1)


def _dup_rope_gain(g):
    g1, g2 = g[NOPE_DIM:NOPE_DIM + HALF_ROPE], g[NOPE_DIM + HALF_ROPE:]
    return jnp.concatenate([g1, g2, g1, g2])[None, :]


def _layer(tok, x, pos, p, attn_fn):
    x = _ffn_call(tok, x, 0, p["g_norm1"], p["wg1"], p["wu1"], p["wd1"])
    uv = _uv_call(tok, x, p["g_norm2"], p["wu"], p["wv"], p["g_v_a"],
                  p["ws_s"] if tok.sample else p["ws_p"], p["bs_s"] if tok.sample else p["bs_p"])
    a_out = uv[0]
    tk = _rope_table(pos)
    cq, ckv, krd = _qkr_call(tok, x, p["g_norm2"], p["wb"], p["g_q_a"], p["g_kv_a"], tk)
    m_out = attn_fn(cq, ckv, krd, tk)
    x = _mixout_call(tok, a_out, m_out, p["g_out_a"], p["g_out_m"], x, p["w_out"])
    x = _ffn_call(tok, x, 6, p["g_norm3"], p["wg2"], p["wu2"], p["wd2"])
    return x, ckv, krd[:, :ROPE_DIM], (uv[1] if tok.sample else None)


def kernel(x_prompt, x_sample, cache_ckv, cache_krope, page_table, c_prompt, c_sample, w_ada, b_ada, g_norm1, w_ff1_gate, w_ff1_up, w_ff1_down, g_norm2, w_in, w_s, b_s, g_v_a, g_q_a, w_q_b, g_kv_a, w_kv_b, g_q_norm, g_k_norm, g_out_a, g_out_m, w_out, g_norm3, w_ff2_gate, w_ff2_up, w_ff2_down):
    n_seq_p, seq_len, _ = x_prompt.shape
    n_seq_s, dec_len, _ = x_sample.shape
    n_pages = page_table.shape[1]
    past = n_pages * PAGE_SIZE
    depth = w_ada.shape[0]
    assert depth == 1 and dec_len == 8

    w_in0 = w_in[0]
    o = 2 * D_A
    w_kr = w_in0[:, o + Q_LORA + KV_LORA:]
    w1, w2 = w_kr[:, :HALF_ROPE], w_kr[:, HALF_ROPE:]
    wq = w_q_b[0]
    qn, q1, q2 = wq[..., :NOPE_DIM], wq[..., NOPE_DIM:NOPE_DIM + HALF_ROPE], wq[..., NOPE_DIM + HALF_ROPE:]
    wkv = w_kv_b[0]
    wk_n, wv_n = wkv[..., :NOPE_DIM], wkv[..., NOPE_DIM:]
    gq, gk = g_q_norm[0], g_k_norm[0]
    tril = jnp.tril(jnp.ones((CHUNK, CHUNK), dtype=bool))
    ws_t = jnp.where(tril[None], w_s[0], 0.0)
    p = {
        "g_norm1": g_norm1, "g_norm2": g_norm2, "g_norm3": g_norm3,
        "wg1": w_ff1_gate[0].astype(BF16), "wu1": w_ff1_up[0].astype(BF16), "wd1": w_ff1_down[0].astype(BF16),
        "wg2": w_ff2_gate[0].astype(BF16), "wu2": w_ff2_up[0].astype(BF16), "wd2": w_ff2_down[0].astype(BF16),
        "wu": w_in0[:, :D_A].astype(BF16), "wv": w_in0[:, D_A:o].astype(BF16),
        "wb": jnp.concatenate([w_in0[:, o:o + Q_LORA + KV_LORA], w1, w2, -w2, w1], axis=1).astype(BF16),
        "g_v_a": g_v_a, "g_q_a": g_q_a, "g_kv_a": g_kv_a,
        "ws_p": ws_t.astype(BF16),
        "bs_p": jnp.broadcast_to(b_s[0][:, :, None], (H_A, CHUNK, HEAD_A)),
        "ws_s": jnp.repeat(ws_t[:, :dec_len, :dec_len].transpose(2, 1, 0), HEAD_A, axis=2),
        "bs_s": jnp.repeat(b_s[0][:, :dec_len].T, HEAD_A, axis=1),
        "g_out_a": g_out_a, "g_out_m": g_out_m, "w_out": w_out[0].astype(BF16),
    }
    wqb = jnp.concatenate([qn, q1, q2, -q2, q1], axis=-1).reshape(Q_LORA, H_M * QH_W).astype(BF16)
    wkvb = jnp.concatenate([wk_n.reshape(KV_LORA, -1), wv_n.reshape(KV_LORA, -1)], axis=1).astype(BF16)
    wkbt =wk_n.transpose(1, 2, 0).astype(BF16)
    wvb = wv_n.transpose(1, 0, 2).astype(BF16)
    gq_dup, gk_dup = _dup_rope_gain(gq), _dup_rope_gain(gk)
    gq_n, gk_n = gq[None, :NOPE_DIM], gk[None, :NOPE_DIM]

    n_c = n_seq_p + n_seq_s
    pad = (-n_c) % 8
    c_all = jnp.concatenate([c_prompt, c_sample, jnp.zeros((pad, D_MODEL), F32)], axis=0)
    mod = _mod_call(c_all, w_ada[0], b_ada)
    mod_p = mod[:n_seq_p].reshape(n_seq_p * N_MOD, 1, D_MODEL)
    mod_s = mod[n_seq_p:n_c]

    tok_p = _Tokens(False, n_seq_p * seq_len, seq_len, mod_p)

    def attn_prompt(cq, ckv, krd, tk):
        q, k, v = _qpost_call(tok_p, cq, ckv, krd, tk * gq_dup, gq_n, gk_n, gk_dup, wqb, wkvb, None)
        return _attn_p_call(q, k, v, n_seq_p, seq_len)

    yp, ckv_p, kr_p, _ = _layer(tok_p, x_prompt.reshape(-1, D_MODEL), jnp.arange(seq_len), p, attn_prompt)

    tok_s = _Tokens(True, n_seq_s * dec_len, dec_len, mod_s)
    cache_krope_t = jnp.swapaxes(cache_krope, 2, 3)

    def attn_sample(cq, ckv, krd, tk):
        qa = _qpost_call(tok_s, cq, None, None, tk * (gq_dup * gk_dup), gq_n * gk_n, None, None, wqb, None, wkbt)
        krt_new = jnp.swapaxes(krd[:, :ROPE_DIM].reshape(n_seq_s, dec_len, ROPE_DIM), 1, 2)
        krt_new = jnp.pad(krt_new, ((0, 0), (0, 0), (0, PAGE_SIZE - dec_len)))
        ctx = _attn_s_call(page_table, qa, ckv.reshape(n_seq_s, dec_len, KV_LORA), krt_new,
                           cache_ckv, cache_krope_t, wkbt.reshape(H_M * NOPE_DIM, KV_LORA))
        return _vexp_call(ctx.reshape(n_seq_s, H_M, dec_len, KV_LORA), wvb)

    pos_s = jnp.tile(past + jnp.arange(dec_len), n_seq_s)
    ys, ckv_s, kr_s, va_s = _layer(tok_s, x_sample.reshape(-1, D_MODEL), pos_s, p, attn_sample)

    return (
        yp.reshape(n_seq_p, seq_len, D_MODEL),
        ys.reshape(n_seq_s, dec_len, D_MODEL),
        ckv_p.reshape(1, n_seq_p, seq_len, KV_LORA),
        kr_p.reshape(1, n_seq_p, seq_len, ROPE_DIM),
        ckv_s.reshape(1, n_seq_s, dec_len, KV_LORA),
        kr_s.reshape(1, n_seq_s, dec_len, ROPE_DIM),
        va_s.reshape(1, n_seq_s, dec_len, H_A, HEAD_A),
    )
```

```python
import functools

import jax
import jax.numpy as jnp
from jax import lax
from jax.experimental import pallas as pl
from jax.experimental.pallas import tpu as pltpu

F32 = jnp.float32
BF16 = jnp.bfloat16

D_MODEL = 4096
D_A = 2048
HEAD_A = 128
H_A = 16
CHUNK = 128
D_M = 2048
V_DIM = 128
H_M = 16
NOPE_DIM = 128
ROPE_DIM = 64
HALF_ROPE = ROPE_DIM // 2
QK_DIM = NOPE_DIM + ROPE_DIM
Q_LORA = 1024
KV_LORA = 512
ROPE_THETA = 10000.0
D_FF = 11008
N_MOD = 9
PAGE_SIZE = 128
EPS = 1e-6

QH_W = 256
QA_W = KV_LORA + 128
LANES = 128
MIB = 1024 * 1024
ROW_STEP = 16
ROW_UNROLL = 4
SEQ_ROWS = 8
KEY_CHUNK = 256
ATTN_S_PAGES = 8
ATTN_S_BLOCKS = 4
FFN_DOWN_COLS = 512

_NT = (((1,), (1,)), ((), ()))
_TN = (((0,), (0,)), ((), ()))


def _dot(a, b):
    return jnp.dot(a, b, preferred_element_type=F32)


def _params(semantics, vmem_mib):
    return pltpu.CompilerParams(dimension_semantics=semantics, vmem_limit_bytes=vmem_mib * MIB)


def _single(block_shape, index_map):
    return pl.BlockSpec(block_shape, index_map, pipeline_mode=pl.Buffered(1))


def _ada_norm(x, g, sh, sc):
    r = lax.rsqrt(jnp.mean(x * x, axis=-1, keepdims=True) + EPS)
    return (x * r * g) * (1.0 + sc) + sh


def _mod_rows(ref, i):
    if ref.shape[0] == 1:
        return ref[...]
    per = ROW_STEP // SEQ_ROWS
    parts = [jnp.broadcast_to(ref[pl.ds(i * per + k, 1), :], (SEQ_ROWS, ref.shape[1])) for k in range(per)]
    return jnp.concatenate(parts, axis=0)


def _row_loop(n_rows, body):
    def step(i, carry):
        body(i, pl.ds(pl.multiple_of(i * ROW_STEP, ROW_STEP), ROW_STEP))
        return carry

    lax.fori_loop(0, n_rows // ROW_STEP, step, 0, unroll=ROW_UNROLL)


def _ada_norm_rows(h_ref, x_ref, g_ref, sh_ref, sc_ref):
    def body(i, rows):
        h_ref[rows, :] = _ada_norm(x_ref[rows, :], g_ref[...], _mod_rows(sh_ref, i), _mod_rows(sc_ref, i)).astype(BF16)

    _row_loop(x_ref.shape[0], body)


def _rms_scale(x, width):
    return lax.rsqrt(jnp.sum(x * x, axis=-1, keepdims=True) * (1.0 / width) + EPS)


def _gelu(x):
    return jax.nn.gelu(x)


class _Tokens:
    def __init__(self, sample, n_tok, seq_len, mod):
        self.sample = sample
        self.n_tok = n_tok
        self.seq_len = seq_len
        self.mod = mod

    def mod_spec(self, tm, k, width=D_MODEL, col=None):
        per_row = D_MODEL // width
        col = col or (lambda *_: 0)
        if self.sample:
            return pl.BlockSpec((tm // SEQ_ROWS, width), lambda i, *r: (i, k * per_row + col(i, *r)))
        per_seq = self.seq_len // tm
        return pl.BlockSpec((None, 1, width), lambda i, *r: ((i // per_seq) * N_MOD + k, 0, col(i, *r)))

    def pos_spec(self, tm, width):
        if self.sample:
            return pl.BlockSpec((tm, width), lambda i, *_: (i, 0))
        per_seq = self.seq_len // tm
        return pl.BlockSpec((tm, width), lambda i, *_: (i % per_seq, 0))

    def tile(self, want):
        tm = min(want, self.n_tok if self.sample else self.seq_len)
        assert self.n_tok % tm == 0 and (self.sample or self.seq_len % tm == 0)
        return tm


def _mod_kernel(c_ref, w_ref, b_ref, o_ref):
    c = c_ref[...]
    a = (c * jax.nn.sigmoid(c)).astype(BF16)
    o_ref[...] = _dot(a, w_ref[...].astype(BF16)) + b_ref[...]


def _mod_call(c, w_ada, b_ada):
    m, n = c.shape[0], w_ada.shape[1]
    tn = 1024
    return pl.pallas_call(
        _mod_kernel,
        grid=(n // tn,),
        in_specs=[
            pl.BlockSpec((m, D_MODEL), lambda j: (0, 0)),
            pl.BlockSpec((D_MODEL, tn), lambda j: (0, j)),
            pl.BlockSpec((1, tn), lambda j: (0, j)),
        ],
        out_specs=pl.BlockSpec((m, tn), lambda j: (0, j)),
        out_shape=jax.ShapeDtypeStruct((m, n), F32),
        compiler_params=_params(("arbitrary",), 56),
        name="mod",
    )(c, w_ada, b_ada)


def _ffn_kernel(x_ref, sh_ref, sc_ref, gt_ref, g_ref, wg_ref, wu_ref, wd_ref, o_ref, h_ref):
    k = pl.program_id(1)

    @pl.when(k == 0)
    def _():
        _ada_norm_rows(h_ref, x_ref, g_ref, sh_ref, sc_ref)
        o_ref[...] = jnp.zeros_like(o_ref)

    h = h_ref[...]
    g = _dot(h, wg_ref[...])
    u = _dot(h, wu_ref[...])
    a = (g * jax.nn.sigmoid(g) * u).astype(BF16)
    for n0 in range(0, D_MODEL, FFN_DOWN_COLS):
        cols = slice(n0, n0 + FFN_DOWN_COLS)
        o_ref[:, cols] += _dot(a, wd_ref[:, cols])

    @pl.when(k == pl.num_programs(1) - 1)
    def _():
        def body(i, rows):
            o_ref[rows, :] = x_ref[rows, :] + (0.5 * _mod_rows(gt_ref, i)) * o_ref[rows, :]

        _row_loop(o_ref.shape[0], body)


def _ffn_call(tok, x, mod_base, g_norm, wg, wu, wd):
    tm = tok.tile(512 if tok.sample else 1024)
    tf = 256
    n_ff = wg.shape[1]
    return pl.pallas_call(
        _ffn_kernel,
        grid=(tok.n_tok // tm, n_ff // tf),
        in_specs=[
            _single((tm, D_MODEL), lambda i, k: (i, 0)),
            tok.mod_spec(tm, mod_base),
            tok.mod_spec(tm, mod_base + 1),
            tok.mod_spec(tm, mod_base + 2),
            pl.BlockSpec((1, D_MODEL), lambda i, k: (0, 0)),
            pl.BlockSpec((D_MODEL, tf), lambda i, k: (0, k)),
            pl.BlockSpec((D_MODEL, tf), lambda i, k: (0, k)),
            pl.BlockSpec((tf, D_MODEL), lambda i, k: (k, 0)),
        ],
        out_specs=_single((tm, D_MODEL), lambda i, k: (i, 0)),
        out_shape=jax.ShapeDtypeStruct((tok.n_tok, D_MODEL), F32),
        scratch_shapes=[pltpu.VMEM((tm, D_MODEL), BF16)],
        compiler_params=_params(("arbitrary", "arbitrary"), 60),
        name="ffn",
    )(x, tok.mod, tok.mod, tok.mod, g_norm, wg, wu, wd)


def _uv_kernel(x_ref, sh_ref, sc_ref, g_ref, wu_ref, wv_ref, gv_ref, ws_ref, bs_ref, *rest, sample):
    if sample:
        a_ref, va_ref, h_ref = rest
    else:
        a_ref, h_ref = rest
    j = pl.program_id(1)

    @pl.when(j == 0)
    def _():
        _ada_norm_rows(h_ref, x_ref, g_ref, sh_ref, sc_ref)

    h = h_ref[...]
    tm = h.shape[0]
    u = _gelu(_dot(h, wu_ref[...]))
    v = _gelu(_dot(h, wv_ref[...]))
    for hh in range(2):
        cols = slice(hh * HEAD_A, (hh + 1) * HEAD_A)
        vh = v[:, cols]
        vh = vh * _rms_scale(vh, HEAD_A) * gv_ref[...]
        if sample:
            va_ref[:, cols] = vh
            nb = tm // 8
            v3 = vh.reshape(nb, 8, HEAD_A)
            mixed = jnp.broadcast_to(bs_ref[:, cols][None], (nb, 8, HEAD_A))
            for s in range(8):
                row = jnp.broadcast_to(v3[:, s:s + 1, :], (nb, 8, HEAD_A))
                mixed = mixed + row * ws_ref[s, :, cols][None]
            a_ref[:, cols] = u[:, cols] * mixed.reshape(tm, HEAD_A)
        else:
            w = ws_ref[hh]
            vb = vh.astype(BF16)
            for c in range(tm // CHUNK):
                rows = slice(c * CHUNK, (c + 1) * CHUNK)
                mixed = _dot(w, vb[rows]) + bs_ref[hh]
                a_ref[rows, cols] = u[rows, cols] * mixed


def _uv_call(tok, x, g_norm, wu, wv, g_v, ws, bs):
    sample = tok.sample
    tm = tok.tile(512 if sample else 1024)
    if sample:
        ws_spec = pl.BlockSpec((8, 8, 2 * HEAD_A), lambda i, j: (0, 0, j))
        bs_spec = pl.BlockSpec((8, 2 * HEAD_A), lambda i, j: (0, j))
    else:
        assert tm % CHUNK == 0
        ws_spec = pl.BlockSpec((2, CHUNK, CHUNK), lambda i, j: (j, 0, 0))
        bs_spec = pl.BlockSpec((2, CHUNK, HEAD_A), lambda i, j: (j, 0, 0))
    blk = pl.BlockSpec((tm, 2 * HEAD_A), lambda i, j: (i, j))
    out_shape = [jax.ShapeDtypeStruct((tok.n_tok, D_A), F32)]
    out_specs = [blk]
    if sample:
        out_shape.append(jax.ShapeDtypeStruct((tok.n_tok, D_A), F32))
        out_specs.append(blk)
    return pl.pallas_call(
        functools.partial(_uv_kernel, sample=sample),
        grid=(tok.n_tok // tm, H_A // 2),
        in_specs=[
            _single((tm, D_MODEL), lambda i, j: (i, 0)),
            tok.mod_spec(tm, 3),
            tok.mod_spec(tm, 4),
            pl.BlockSpec((1, D_MODEL), lambda i, j: (0, 0)),
            pl.BlockSpec((D_MODEL, 2 * HEAD_A), lambda i, j: (0, j)),
            pl.BlockSpec((D_MODEL, 2 * HEAD_A), lambda i, j: (0, j)),
            pl.BlockSpec((1, HEAD_A), lambda i, j: (0, 0)),
            ws_spec,
            bs_spec,
        ],
        out_specs=out_specs,
        out_shape=out_shape,
        scratch_shapes=[pltpu.VMEM((tm, D_MODEL), BF16)],
        compiler_params=_params(("arbitrary", "arbitrary"), 48),
        name="uv",
    )(x, tok.mod, tok.mod, g_norm, wu, wv, g_v, ws, bs)


def _qkr_kernel(x_ref, sh_ref, sc_ref, g_ref, wb_ref, gq_ref, gkv_ref, tk_ref, cq_ref, ckv_ref, kr_ref,
                h_ref):
    _ada_norm_rows(h_ref, x_ref, g_ref, sh_ref, sc_ref)
    p = _dot(h_ref[...], wb_ref[...])
    cq = p[:, :Q_LORA]
    cq_ref[...] = (cq * _rms_scale(cq, Q_LORA) * gq_ref[...]).astype(BF16)
    ckv = p[:, Q_LORA:Q_LORA + KV_LORA]
    ckv_ref[...] = ckv * _rms_scale(ckv, KV_LORA) * gkv_ref[...]
    e = p[:, Q_LORA + KV_LORA:] * tk_ref[...]
    kr_ref[...] = e + pltpu.roll(e, 2 * HALF_ROPE, axis=1)


def _qkr_call(tok, x, g_norm, wb, g_q_a, g_kv_a, tk):
    tm = tok.tile(512)
    nb = wb.shape[1]
    row = lambda w: pl.BlockSpec((tm, w), lambda i: (i, 0))
    return pl.pallas_call(
        _qkr_kernel,
        grid=(tok.n_tok // tm,),
        in_specs=[
            row(D_MODEL),
            tok.mod_spec(tm, 3),
            tok.mod_spec(tm, 4),
            pl.BlockSpec((1, D_MODEL), lambda i: (0, 0)),
            _single((D_MODEL, nb), lambda i: (0, 0)),
            pl.BlockSpec((1, Q_LORA), lambda i: (0, 0)),
            pl.BlockSpec((1, KV_LORA), lambda i: (0, 0)),
            tok.pos_spec(tm, LANES),
        ],
        out_specs=[row(Q_LORA), row(KV_LORA), row(LANES)],
        out_shape=[
            jax.ShapeDtypeStruct((tok.n_tok, Q_LORA), BF16),
            jax.ShapeDtypeStruct((tok.n_tok, KV_LORA), F32),
            jax.ShapeDtypeStruct((tok.n_tok, LANES), F32),
        ],
        scratch_shapes=[pltpu.VMEM((tm, D_MODEL), BF16)],
        compiler_params=_params(("arbitrary",), 56),
        name="qkr",
    )(x, tok.mod, tok.mod, g_norm, wb, g_q_a, g_kv_a, tk)


def _q_heads(cq_ref, wqb_ref, tq_ref, ga_ref):
    q = _dot(cq_ref[...], wqb_ref[...])
    tq = tq_ref[...]
    ga = ga_ref[...]
    for h in range(H_M):
        a = q[:, h * QH_W:h * QH_W + NOPE_DIM]
        b = q[:, h * QH_W + NOPE_DIM:(h + 1) * QH_W]
        ss = jnp.sum(a * a + 0.5 * (b * b), axis=-1, keepdims=True)
        r = lax.rsqrt(ss * (1.0 / QK_DIM) + EPS) * (QK_DIM ** -0.5)
        yield h, a * ga * r, b * tq * r


def _qpost_p_kernel(cq_ref, ckv_ref, kr_ref, tq_ref, ga_ref, gkn_ref, gkr_ref, wqb_ref, wkvb_ref,
                    q_ref, k_ref, v_ref):
    for h, a, b in _q_heads(cq_ref, wqb_ref, tq_ref, ga_ref):
        q_ref[:, h * QH_W:h * QH_W + NOPE_DIM] = a.astype(BF16)
        q_ref[:, h * QH_W + NOPE_DIM:(h + 1) * QH_W] = b.astype(BF16)
    kv = _dot(ckv_ref[...].astype(BF16), wkvb_ref[...])
    v_ref[...] = kv[:, D_M:].astype(BF16)
    kr = kr_ref[...]
    rr = 0.5 * jnp.sum(kr * kr, axis=-1, keepdims=True)
    krg = kr * gkr_ref[...]
    for h in range(H_M):
        kn = kv[:, h * NOPE_DIM:(h + 1) * NOPE_DIM]
        r = lax.rsqrt((jnp.sum(kn * kn, axis=-1, keepdims=True) + rr) * (1.0 / QK_DIM) + EPS)
        k_ref[:, h * QH_W:h * QH_W + NOPE_DIM] = (kn * gkn_ref[...] * r).astype(BF16)
        k_ref[:, h * QH_W + NOPE_DIM:(h + 1) * QH_W] = (krg * r).astype(BF16)


def _qpost_s_kernel(cq_ref, tq_ref, ga_ref, wqb_ref, wkbt_ref, qa_ref):
    tm = cq_ref.shape[0]
    lane = lax.broadcasted_iota(jnp.int32, (tm, LANES), 1)
    for h, a, b in _q_heads(cq_ref, wqb_ref, tq_ref, ga_ref):
        qabs = _dot(a.astype(BF16), wkbt_ref[h])
        qr = jnp.where(lane < ROPE_DIM, b + pltpu.roll(b, ROPE_DIM, axis=1), 0.0)
        qa_ref[:, h, :, :KV_LORA] = qabs.reshape(tm // 8, 8, KV_LORA)
        qa_ref[:, h, :, KV_LORA:] = qr.reshape(tm // 8, 8, LANES)


def _qpost_call(tok, cq, ckv, kr, tq, ga, gkn, gkr, wqb, wkvb, wkbt):
    tm = tok.tile(256)
    row = lambda w: pl.BlockSpec((tm, w), lambda i: (i, 0))
    one = lambda w: pl.BlockSpec((1, w), lambda i: (0, 0))
    n = tok.n_tok
    if tok.sample:
        return pl.pallas_call(
            _qpost_s_kernel,
            grid=(n // tm,),
            in_specs=[
                row(Q_LORA), tok.pos_spec(tm, LANES), one(NOPE_DIM),
                _single((Q_LORA, H_M * QH_W), lambda i: (0, 0)),
                _single((H_M, NOPE_DIM, KV_LORA), lambda i: (0, 0, 0)),
            ],
            out_specs=pl.BlockSpec((tm // 8, H_M, 8, QA_W), lambda i: (i, 0, 0, 0)),
            out_shape=jax.ShapeDtypeStruct((n // 8, H_M, 8, QA_W), F32),
            compiler_params=_params(("arbitrary",), 56),
            name="qpost_s",
        )(cq, tq, ga, wqb, wkbt)
    return pl.pallas_call(
        _qpost_p_kernel,
        grid=(n // tm,),
        in_specs=[
            row(Q_LORA), row(KV_LORA), row(LANES), tok.pos_spec(tm, LANES),
            one(NOPE_DIM), one(NOPE_DIM), one(LANES),
            _single((Q_LORA, H_M * QH_W), lambda i: (0, 0)),
            _single((KV_LORA, 2 * D_M), lambda i: (0, 0)),
        ],
        out_specs=[row(H_M * QH_W), row(H_M * QH_W), row(D_M)],
        out_shape=[
            jax.ShapeDtypeStruct((n, H_M * QH_W), BF16),
            jax.ShapeDtypeStruct((n, H_M * QH_W), BF16),
            jax.ShapeDtypeStruct((n, D_M), BF16),
        ],
        compiler_params=_params(("arbitrary",), 56),
        name="qpost_p",
    )(cq, ckv, kr, tq, ga, gkn, gkr, wqb, wkvb)


def _attn_p_kernel(q_ref, k_ref, v_ref, o_ref):
    tq = q_ref.shape[0]
    qi = pl.program_id(2)
    for n in range(k_ref.shape[0] // tq):
        @pl.when(qi == n)
        def _(n=n):
            nk = (n + 1) * tq
            s = lax.dot_general(q_ref[...], k_ref[0:nk, :], _NT, preferred_element_type=F32)
            q_pos = n * tq + lax.broadcasted_iota(jnp.int32, (tq, nk), 0)
            k_pos = lax.broadcasted_iota(jnp.int32, (tq, nk), 1)
            s = jnp.where(k_pos <= q_pos, s, -jnp.inf)
            e = jnp.exp(s - jnp.max(s, axis=-1, keepdims=True))
            p = e * (1.0 / jnp.sum(e, axis=-1, keepdims=True))
            o_ref[...] = _dot(p.astype(BF16), v_ref[0:nk, :])


def _attn_p_call(q, k, v, n_seq, seq_len):
    tq = min(512, seq_len)
    nq = seq_len // tq
    return pl.pallas_call(
        _attn_p_kernel,
        grid=(n_seq, H_M, nq),
        in_specs=[
            pl.BlockSpec((tq, QH_W), lambda b, h, i: (b * nq + i, h)),
            pl.BlockSpec((seq_len, QH_W), lambda b, h, i: (b, h)),
            pl.BlockSpec((seq_len, V_DIM), lambda b, h, i: (b, h)),
        ],
        out_specs=pl.BlockSpec((tq, V_DIM), lambda b, h, i: (b * nq + i, h)),
        out_shape=jax.ShapeDtypeStruct((n_seq * seq_len, D_M), F32),
        compiler_params=_params(("arbitrary", "arbitrary", "arbitrary"), 48),
        name="attn_p",
    )(q, k, v)


def _attn_s_kernel(pt_ref, q_ref, cn_ref, rnt_ref, wkbt_ref, *rest, npb, nblk):
    del pt_ref
    n_pg = npb * nblk
    c_refs, r_refs = rest[:n_pg], rest[n_pg:2 * n_pg]
    o_ref, lhs_ref, qr_ref = rest[2 * n_pg:2 * n_pg + 3]
    slots = rest[2 * n_pg + 3:2 * n_pg + 3 + 3 * nblk]
    cbs, rts, ss = slots[0::3], slots[1::3], slots[2::3]
    m_ref, l_ref, acc_ref = rest[2 * n_pg + 3 + 3 * nblk:]
    j = pl.program_id(1)
    nq = q_ref.shape[1]
    ncol = H_M * nq
    nkd = H_M * NOPE_DIM
    mk = npb * PAGE_SIZE

    @pl.when(j == 0)
    def _():
        m_ref[...] = jnp.full_like(m_ref, -jnp.inf)
        l_ref[...] = jnp.zeros_like(l_ref)
        acc_ref[...] = jnp.zeros_like(acc_ref)
        q = q_ref[...].reshape(ncol, QA_W)
        lhs_ref[0:nkd, :] = wkbt_ref[...]
        lhs_ref[nkd:, :] = q[:, :KV_LORA].astype(BF16)
        qr_ref[...] = q[:, KV_LORA:].astype(BF16)
        for rt in rts:
            rt[ROPE_DIM:, :] = jnp.zeros((rt.shape[0] - ROPE_DIM, mk), BF16)

    def scores(cb, rt, s_out, n_keys, stage):
        chunk = min(KEY_CHUNK, n_keys)
        for n0 in range(0, n_keys, chunk):
            keys = slice(n0, n0 + chunk)
            rr = stage(n0, chunk)
            r = lax.dot_general(lhs_ref[...], cb[keys, :], _NT, preferred_element_type=F32)
            kn = r[:nkd].reshape(H_M, NOPE_DIM // 8, 8, chunk)
            tot = jnp.sum(jnp.sum(kn * kn, axis=1), axis=1, keepdims=True)
            s3 = (r[nkd:] + _dot(qr_ref[...], rt[:, keys])).reshape(H_M, nq, chunk)
            inv = lax.rsqrt((tot + rr[None]) * (1.0 / QK_DIM) + EPS)
            s_out[:, keys] = (s3 * inv).reshape(ncol, chunk)

    def fold(cb, s_in, n_keys, mask):
        s = s_in[:, 0:n_keys]
        if mask is not None:
            s = jnp.where(mask, s, -jnp.inf)
        m_old = m_ref[...]
        m_new = jnp.maximum(m_old, jnp.max(s, axis=1, keepdims=True))
        alpha = jnp.exp(m_old - m_new)
        p = jnp.exp(s - m_new)
        l_ref[...] = alpha * l_ref[...] + jnp.sum(p, axis=1, keepdims=True)
        m_ref[...] = m_new
        acc_ref[...] = alpha * acc_ref[...] + _dot(p.astype(BF16), cb[0:n_keys, :])

    def stage_pages(k):
        def stage(n0, n):
            rrs = []
            for i in range(n0 // PAGE_SIZE, (n0 + n) // PAGE_SIZE):
                keys = slice(i * PAGE_SIZE, (i + 1) * PAGE_SIZE)
                cbs[k][keys, :] = c_refs[k * npb + i][...].astype(BF16)
                r = r_refs[k * npb + i][...]
                rts[k][0:ROPE_DIM, keys] = r.astype(BF16)
                rrs.append(jnp.sum(r * r, axis=0, keepdims=True))
            return rrs[0] if len(rrs) == 1 else jnp.concatenate(rrs, axis=1)

        return stage

    scores(cbs[0], rts[0], ss[0], mk, stage_pages(0))
    for k in range(1, nblk):
        scores(cbs[k], rts[k], ss[k], mk, stage_pages(k))
        fold(cbs[k - 1], ss[k - 1], mk, None)
    fold(cbs[nblk - 1], ss[nblk - 1], mk, None)

    @pl.when(j == pl.num_programs(1) - 1)
    def _():
        def stage_new(n0, n):
            pad = jnp.zeros((n - nq, KV_LORA), F32)
            cbs[0][0:n, :] = jnp.concatenate([cn_ref[...], pad], axis=0).astype(BF16)
            rn = rnt_ref[...]
            rts[0][0:ROPE_DIM, 0:n] = rn.astype(BF16)
            return jnp.sum(rn * rn, axis=0, keepdims=True)

        scores(cbs[0], rts[0], ss[0], PAGE_SIZE, stage_new)
        t = lax.broadcasted_iota(jnp.int32, (ncol, PAGE_SIZE), 1)
        qq = lax.broadcasted_iota(jnp.int32, (ncol, PAGE_SIZE), 0) % nq
        fold(cbs[0], ss[0], PAGE_SIZE, t <= qq)
        o_ref[...] = acc_ref[...] * (1.0 / l_ref[...])


def _attn_s_call(page_table, qa, ckv_new, krt_new, cache_ckv, cache_krope_t, wkbt):
    n_seq, n_pages = page_table.shape
    nq = qa.shape[2]
    assert nq == SEQ_ROWS
    nblk = max(d for d in range(1, ATTN_S_BLOCKS + 1) if n_pages % d == 0)
    npb = max(d for d in range(1, ATTN_S_PAGES + 1) if (n_pages // nblk) % d == 0)
    n_pg = npb * nblk
    ncol = H_M * nq
    nkd = H_M * NOPE_DIM
    mk = npb * PAGE_SIZE

    def page(i):
        return lambda b, j, pt: (0, pt[b * n_pages + j * n_pg + i], 0, 0)

    c_specs = [pl.BlockSpec((None, None, PAGE_SIZE, KV_LORA), page(i)) for i in range(n_pg)]
    r_specs = [pl.BlockSpec((None, None, ROPE_DIM, PAGE_SIZE), page(i)) for i in range(n_pg)]
    slot_scratch = [
        pltpu.VMEM((mk, KV_LORA), BF16),
        pltpu.VMEM((LANES, mk), BF16),
        pltpu.VMEM((ncol, mk), F32),
    ]
    grid_spec = pltpu.PrefetchScalarGridSpec(
        num_scalar_prefetch=1,
        grid=(n_seq, n_pages // n_pg),
        in_specs=[
            pl.BlockSpec((None, H_M, nq, QA_W), lambda b, j, pt: (b, 0, 0, 0)),
            pl.BlockSpec((None, nq, KV_LORA), lambda b, j, pt: (b, 0, 0)),
            pl.BlockSpec((None, ROPE_DIM, PAGE_SIZE), lambda b, j, pt: (b, 0, 0)),
            pl.BlockSpec((nkd, KV_LORA), lambda b, j, pt: (0, 0)),
        ] + c_specs + r_specs,
        out_specs=pl.BlockSpec((None, ncol, KV_LORA), lambda b, j, pt: (b, 0, 0)),
        scratch_shapes=[
            pltpu.VMEM((nkd + ncol, KV_LORA), BF16),
            pltpu.VMEM((ncol, LANES), BF16),
        ] + slot_scratch * nblk + [
            pltpu.VMEM((ncol, 1), F32),
            pltpu.VMEM((ncol, 1), F32),
            pltpu.VMEM((ncol, KV_LORA), F32),
        ],
    )
    return pl.pallas_call(
        functools.partial(_attn_s_kernel, npb=npb, nblk=nblk),
        grid_spec=grid_spec,
        out_shape=jax.ShapeDtypeStruct((n_seq, ncol, KV_LORA), F32),
        compiler_params=_params(("arbitrary", "arbitrary"), 56),
        name="attn_s",
    )(page_table.reshape(-1), qa, ckv_new, krt_new, wkbt,
      *([cache_ckv] * n_pg), *([cache_krope_t] * n_pg))


def _vexp_kernel(ctx_ref, wv_ref, o_ref):
    nb, nq, _ = ctx_ref.shape
    o_ref[...] = _dot(ctx_ref[...].reshape(nb * nq, KV_LORA).astype(BF16), wv_ref[...])


def _vexp_call(ctx, wvb):
    n_seq, _, nq, _ = ctx.shape
    return pl.pallas_call(
        _vexp_kernel,
        grid=(H_M,),
        in_specs=[
            pl.BlockSpec((n_seq, None, nq, KV_LORA), lambda h: (0, h, 0, 0)),
            pl.BlockSpec((None, KV_LORA, V_DIM), lambda h: (h, 0, 0)),
        ],
        out_specs=pl.BlockSpec((n_seq * nq, V_DIM), lambda h: (0, h)),
        out_shape=jax.ShapeDtypeStruct((n_seq * nq, D_M), F32),
        compiler_params=_params(("arbitrary",), 32),
        name="vexp",
    )(ctx, wvb)


def _mixout_kernel(a_ref, m_ref, ga_ref, gm_ref, x_ref, gt_ref, w_ref, o_ref, lhs_ref):
    @pl.when(pl.program_id(1) == 0)
    def _():
        def body(i, rows):
            a = a_ref[rows, :]
            lhs_ref[rows, :D_A] = (a * _rms_scale(a, D_A) * ga_ref[...]).astype(BF16)
            m = m_ref[rows, :]
            lhs_ref[rows, D_A:] = (m * _rms_scale(m, D_M) * gm_ref[...]).astype(BF16)

        _row_loop(a_ref.shape[0], body)

    y = _dot(lhs_ref[...], w_ref[...])
    if gt_ref.shape[0] == 1:
        o_ref[...] = x_ref[...] + gt_ref[...] * y
    else:
        o_ref[...] = y

        def body(i, rows):
            o_ref[rows, :] = x_ref[rows, :] + _mod_rows(gt_ref, i) * o_ref[rows, :]

        _row_loop(o_ref.shape[0], body)


def _mixout_call(tok, a, m, g_a, g_m, x, w_out):
    tm = tok.tile(512)
    tn = 512
    gt_spec = tok.mod_spec(tm, 5, width=tn, col=lambda i, j: j)
    return pl.pallas_call(
        _mixout_kernel,
        grid=(tok.n_tok // tm, D_MODEL // tn),
        in_specs=[
            pl.BlockSpec((tm, D_A), lambda i, j: (i, 0)),
            pl.BlockSpec((tm, D_M), lambda i, j: (i, 0)),
            pl.BlockSpec((1, D_A), lambda i, j: (0, 0)),
            pl.BlockSpec((1, D_M), lambda i, j: (0, 0)),
            pl.BlockSpec((tm, tn), lambda i, j: (i, j)),
            gt_spec,
            pl.BlockSpec((D_A + D_M, tn), lambda i, j: (0, j)),
        ],
        out_specs=pl.BlockSpec((tm, tn), lambda i, j: (i, j)),
        out_shape=jax.ShapeDtypeStruct((tok.n_tok, D_MODEL), F32),
        scratch_shapes=[pltpu.VMEM((tm, D_A + D_M), BF16)],
        compiler_params=_params(("arbitrary", "arbitrary"), 48),
        name="mixout",
    )(a, m, g_a, g_m, x, tok.mod, w_out)


def _rope_table(pos):
    inv_freq = ROPE_THETA ** (-jnp.arange(HALF_ROPE, dtype=F32) / HALF_ROPE)
    ang = pos.astype(F32)[:, None] * inv_freq[None, :]
    cos, sin = jnp.cos(ang), jnp.sin(ang)
    return jnp.concatenate([cos, cos, sin, sin], axis=-1)


def _dup_rope_gain(g):
    g1, g2 = g[NOPE_DIM:NOPE_DIM + HALF_ROPE], g[NOPE_DIM + HALF_ROPE:]
    return jnp.concatenate([g1, g2, g1, g2])[None, :]


def _layer(tok, x, pos, p, attn_fn):
    x = _ffn_call(tok, x, 0, p["g_norm1"], p["wg1"], p["wu1"], p["wd1"])
    uv = _uv_call(tok, x, p["g_norm2"], p["wu"], p["wv"], p["g_v_a"],
                  p["ws_s"] if tok.sample else p["ws_p"], p["bs_s"] if tok.sample else p["bs_p"])
    a_out = uv[0]
    tk = _rope_table(pos)
    cq, ckv, krd = _qkr_call(tok, x, p["g_norm2"], p["wb"], p["g_q_a"], p["g_kv_a"], tk)
    m_out = attn_fn(cq, ckv, krd, tk)
    x = _mixout_call(tok, a_out, m_out, p["g_out_a"], p["g_out_m"], x, p["w_out"])
    x = _ffn_call(tok, x, 6, p["g_norm3"], p["wg2"], p["wu2"], p["wd2"])
    return x, ckv, krd[:, :ROPE_DIM], (uv[1] if tok.sample else None)


def kernel(x_prompt, x_sample, cache_ckv, cache_krope, page_table, c_prompt, c_sample, w_ada, b_ada, g_norm1, w_ff1_gate, w_ff1_up, w_ff1_down, g_norm2, w_in, w_s, b_s, g_v_a, g_q_a, w_q_b, g_kv_a, w_kv_b, g_q_norm, g_k_norm, g_out_a, g_out_m, w_out, g_norm3, w_ff2_gate, w_ff2_up, w_ff2_down):
    n_seq_p, seq_len, _ = x_prompt.shape
    n_seq_s, dec_len, _ = x_sample.shape
    n_pages = page_table.shape[1]
    past = n_pages * PAGE_SIZE
    depth = w_ada.shape[0]
    assert depth == 1 and dec_len == 8

    w_in0 = w_in[0]
    o = 2 * D_A
    w_kr = w_in0[:, o + Q_LORA + KV_LORA:]
    w1, w2 = w_kr[:, :HALF_ROPE], w_kr[:, HALF_ROPE:]
    wq = w_q_b[0]
    qn, q1, q2 = wq[..., :NOPE_DIM], wq[..., NOPE_DIM:NOPE_DIM + HALF_ROPE], wq[..., NOPE_DIM + HALF_ROPE:]
    wkv = w_kv_b[0]
    wk_n, wv_n = wkv[..., :NOPE_DIM], wkv[..., NOPE_DIM:]
    gq, gk = g_q_norm[0], g_k_norm[0]
    tril = jnp.tril(jnp.ones((CHUNK, CHUNK), dtype=bool))
    ws_t = jnp.where(tril[None], w_s[0], 0.0)
    p = {
        "g_norm1": g_norm1, "g_norm2": g_norm2, "g_norm3": g_norm3,
        "wg1": w_ff1_gate[0].astype(BF16), "wu1": w_ff1_up[0].astype(BF16), "wd1": w_ff1_down[0].astype(BF16),
        "wg2": w_ff2_gate[0].astype(BF16), "wu2": w_ff2_up[0].astype(BF16), "wd2": w_ff2_down[0].astype(BF16),
        "wu": w_in0[:, :D_A].astype(BF16), "wv": w_in0[:, D_A:o].astype(BF16),
        "wb": jnp.concatenate([w_in0[:, o:o + Q_LORA + KV_LORA], w1, w2, -w2, w1], axis=1).astype(BF16),
        "g_v_a": g_v_a, "g_q_a": g_q_a, "g_kv_a": g_kv_a,
        "ws_p": ws_t.astype(BF16),
        "bs_p": jnp.broadcast_to(b_s[0][:, :, None], (H_A, CHUNK, HEAD_A)),
        "ws_s": jnp.repeat(ws_t[:, :dec_len, :dec_len].transpose(2, 1, 0), HEAD_A, axis=2),
        "bs_s": jnp.repeat(b_s[0][:, :dec_len].T, HEAD_A, axis=1),
        "g_out_a": g_out_a, "g_out_m": g_out_m, "w_out": w_out[0].astype(BF16),
    }
    wqb = jnp.concatenate([qn, q1, q2, -q2, q1], axis=-1).reshape(Q_LORA, H_M * QH_W).astype(BF16)
    wkvb = jnp.concatenate([wk_n.reshape(KV_LORA, -1), wv_n.reshape(KV_LORA, -1)], axis=1).astype(BF16)
    wkbt =wk_n.transpose(1, 2, 0).astype(BF16)
    wvb = wv_n.transpose(1, 0, 2).astype(BF16)
    gq_dup, gk_dup = _dup_rope_gain(gq), _dup_rope_gain(gk)
    gq_n, gk_n = gq[None, :NOPE_DIM], gk[None, :NOPE_DIM]

    n_c = n_seq_p + n_seq_s
    pad = (-n_c) % 8
    c_all = jnp.concatenate([c_prompt, c_sample, jnp.zeros((pad, D_MODEL), F32)], axis=0)
    mod = _mod_call(c_all, w_ada[0], b_ada)
    mod_p = mod[:n_seq_p].reshape(n_seq_p * N_MOD, 1, D_MODEL)
    mod_s = mod[n_seq_p:n_c]

    tok_p = _Tokens(False, n_seq_p * seq_len, seq_len, mod_p)

    def attn_prompt(cq, ckv, krd, tk):
        q, k, v = _qpost_call(tok_p, cq, ckv, krd, tk * gq_dup, gq_n, gk_n, gk_dup, wqb, wkvb, None)
        return _attn_p_call(q, k, v, n_seq_p, seq_len)

    yp, ckv_p, kr_p, _ = _layer(tok_p, x_prompt.reshape(-1, D_MODEL), jnp.arange(seq_len), p, attn_prompt)

    tok_s = _Tokens(True, n_seq_s * dec_len, dec_len, mod_s)
    cache_krope_t = jnp.swapaxes(cache_krope, 2, 3)

    def attn_sample(cq, ckv, krd, tk):
        qa = _qpost_call(tok_s, cq, None, None, tk * (gq_dup * gk_dup), gq_n * gk_n, None, None, wqb, None, wkbt)
        krt_new = jnp.swapaxes(krd[:, :ROPE_DIM].reshape(n_seq_s, dec_len, ROPE_DIM), 1, 2)
        krt_new = jnp.pad(krt_new, ((0, 0), (0, 0), (0, PAGE_SIZE - dec_len)))
        ctx = _attn_s_call(page_table, qa, ckv.reshape(n_seq_s, dec_len, KV_LORA), krt_new,
                           cache_ckv, cache_krope_t, wkbt.reshape(H_M * NOPE_DIM, KV_LORA))
        return _vexp_call(ctx.reshape(n_seq_s, H_M, dec_len, KV_LORA), wvb)

    pos_s = jnp.tile(past + jnp.arange(dec_len), n_seq_s)
    ys, ckv_s, kr_s, va_s = _layer(tok_s, x_sample.reshape(-1, D_MODEL), pos_s, p, attn_sample)

    return (
        yp.reshape(n_seq_p, seq_len, D_MODEL),
        ys.reshape(n_seq_s, dec_len, D_MODEL),
        ckv_p.reshape(1, n_seq_p, seq_len, KV_LORA),
        kr_p.reshape(1, n_seq_p, seq_len, ROPE_DIM),
        ckv_s.reshape(1, n_seq_s, dec_len, KV_LORA),
        kr_s.reshape(1, n_seq_s, dec_len, ROPE_DIM),
        va_s.reshape(1, n_seq_s, dec_len, H_A, HEAD_A),
    )
```

```python
import functools

import jax
import jax.numpy as jnp
from jax import lax
from jax.experimental import pallas as pl
from jax.experimental.pallas import tpu as pltpu

F32 = jnp.float32
BF16 = jnp.bfloat16

D_MODEL = 4096
D_A = 2048
HEAD_A = 128
H_A = 16
CHUNK = 128
D_M = 2048
V_DIM = 128
H_M = 16
NOPE_DIM = 128
ROPE_DIM = 64
HALF_ROPE = ROPE_DIM // 2
QK_DIM = NOPE_DIM + ROPE_DIM
Q_LORA = 1024
KV_LORA = 512
ROPE_THETA = 10000.0
D_FF = 11008
N_MOD = 9
PAGE_SIZE = 128
EPS = 1e-6

QH_W = 256
QA_W = KV_LORA + 128
LANES = 128
MIB = 1024 * 1024
ROW_STEP = 16
ROW_UNROLL = 4
SEQ_ROWS = 8
KEY_CHUNK = 256
ATTN_P_HEADS = 4
ATTN_S_PAGES = 8
ATTN_S_BLOCKS = 4
FFN_DOWN_COLS = 512

_NT = (((1,), (1,)), ((), ()))
_TN = (((0,), (0,)), ((), ()))


def _dot(a, b):
    return jnp.dot(a, b, preferred_element_type=F32)


def _params(semantics, vmem_mib):
    return pltpu.CompilerParams(dimension_semantics=semantics, vmem_limit_bytes=vmem_mib * MIB)


def _single(block_shape, index_map):
    return pl.BlockSpec(block_shape, index_map, pipeline_mode=pl.Buffered(1))


def _ada_norm(x, g, sh, sc):
    r = lax.rsqrt(jnp.mean(x * x, axis=-1, keepdims=True) + EPS)
    return (x * r * g) * (1.0 + sc) + sh


def _mod_rows(ref, i):
    if ref.shape[0] == 1:
        return ref[...]
    per = ROW_STEP // SEQ_ROWS
    parts = [jnp.broadcast_to(ref[pl.ds(i * per + k, 1), :], (SEQ_ROWS, ref.shape[1])) for k in range(per)]
    return jnp.concatenate(parts, axis=0)


def _row_loop(n_rows, body):
    def step(i, carry):
        body(i, pl.ds(pl.multiple_of(i * ROW_STEP, ROW_STEP), ROW_STEP))
        return carry

    lax.fori_loop(0, n_rows // ROW_STEP, step, 0, unroll=ROW_UNROLL)


def _ada_norm_rows(h_ref, x_ref, g_ref, sh_ref, sc_ref):
    def body(i, rows):
        h_ref[rows, :] = _ada_norm(x_ref[rows, :], g_ref[...], _mod_rows(sh_ref, i), _mod_rows(sc_ref, i)).astype(BF16)

    _row_loop(x_ref.shape[0], body)


def _rms_scale(x, width):
    return lax.rsqrt(jnp.sum(x * x, axis=-1, keepdims=True) * (1.0 / width) + EPS)


def _gelu(x):
    return jax.nn.gelu(x)


class _Tokens:
    def __init__(self, sample, n_tok, seq_len, mod):
        self.sample = sample
        self.n_tok = n_tok
        self.seq_len = seq_len
        self.mod = mod

    def mod_spec(self, tm, k, width=D_MODEL, col=None):
        per_row = D_MODEL // width
        col = col or (lambda *_: 0)
        if self.sample:
            return pl.BlockSpec((tm // SEQ_ROWS, width), lambda i, *r: (i, k * per_row + col(i, *r)))
        per_seq = self.seq_len // tm
        return pl.BlockSpec((None, 1, width), lambda i, *r: ((i // per_seq) * N_MOD + k, 0, col(i, *r)))

    def pos_spec(self, tm, width):
        if self.sample:
            return pl.BlockSpec((tm, width), lambda i, *_: (i, 0))
        per_seq = self.seq_len // tm
        return pl.BlockSpec((tm, width), lambda i, *_: (i % per_seq, 0))

    def tile(self, want):
        tm = min(want, self.n_tok if self.sample else self.seq_len)
        assert self.n_tok % tm == 0 and (self.sample or self.seq_len % tm == 0)
        return tm


def _mod_kernel(c_ref, w_ref, b_ref, o_ref):
    c = c_ref[...]
    a = (c * jax.nn.sigmoid(c)).astype(BF16)
    o_ref[...] = _dot(a, w_ref[...].astype(BF16)) + b_ref[...]


def _mod_call(c, w_ada, b_ada):
    m, n = c.shape[0], w_ada.shape[1]
    tn = 1024
    return pl.pallas_call(
        _mod_kernel,
        grid=(n // tn,),
        in_specs=[
            pl.BlockSpec((m, D_MODEL), lambda j: (0, 0)),
            pl.BlockSpec((D_MODEL, tn), lambda j: (0, j)),
            pl.BlockSpec((1, tn), lambda j: (0, j)),
        ],
        out_specs=pl.BlockSpec((m, tn), lambda j: (0, j)),
        out_shape=jax.ShapeDtypeStruct((m, n), F32),
        compiler_params=_params(("arbitrary",), 56),
        name="mod",
    )(c, w_ada, b_ada)


def _ffn_kernel(x_ref, sh_ref, sc_ref, gt_ref, g_ref, wg_ref, wu_ref, wd_ref, o_ref, h_ref):
    k = pl.program_id(1)

    @pl.when(k == 0)
    def _():
        _ada_norm_rows(h_ref, x_ref, g_ref, sh_ref, sc_ref)
        o_ref[...] = jnp.zeros_like(o_ref)

    h = h_ref[...]
    g = _dot(h, wg_ref[...])
    u = _dot(h, wu_ref[...])
    a = (g * jax.nn.sigmoid(g) * u).astype(BF16)
    for n0 in range(0, D_MODEL, FFN_DOWN_COLS):
        cols = slice(n0, n0 + FFN_DOWN_COLS)
        o_ref[:, cols] += _dot(a, wd_ref[:, cols])

    @pl.when(k == pl.num_programs(1) - 1)
    def _():
        def body(i, rows):
            o_ref[rows, :] = x_ref[rows, :] + (0.5 * _mod_rows(gt_ref, i)) * o_ref[rows, :]

        _row_loop(o_ref.shape[0], body)


def _ffn_call(tok, x, mod_base, g_norm, wg, wu, wd):
    tm = tok.tile(512 if tok.sample else 1024)
    tf = 256
    n_ff = wg.shape[1]
    return pl.pallas_call(
        _ffn_kernel,
        grid=(tok.n_tok // tm, n_ff // tf),
        in_specs=[
            _single((tm, D_MODEL), lambda i, k: (i, 0)),
            tok.mod_spec(tm, mod_base),
            tok.mod_spec(tm, mod_base + 1),
            tok.mod_spec(tm, mod_base + 2),
            pl.BlockSpec((1, D_MODEL), lambda i, k: (0, 0)),
            pl.BlockSpec((D_MODEL, tf), lambda i, k: (0, k)),
            pl.BlockSpec((D_MODEL, tf), lambda i, k: (0, k)),
            pl.BlockSpec((tf, D_MODEL), lambda i, k: (k, 0)),
        ],
        out_specs=_single((tm, D_MODEL), lambda i, k: (i, 0)),
        out_shape=jax.ShapeDtypeStruct((tok.n_tok, D_MODEL), F32),
        scratch_shapes=[pltpu.VMEM((tm, D_MODEL), BF16)],
        compiler_params=_params(("arbitrary", "arbitrary"), 60),
        name="ffn",
    )(x, tok.mod, tok.mod, tok.mod, g_norm, wg, wu, wd)


def _uv_kernel(x_ref, sh_ref, sc_ref, g_ref, wu_ref, wv_ref, gv_ref, ws_ref, bs_ref, *rest, sample):
    if sample:
        a_ref, va_ref, h_ref = rest
    else:
        a_ref, h_ref = rest
    j = pl.program_id(1)

    @pl.when(j == 0)
    def _():
        _ada_norm_rows(h_ref, x_ref, g_ref, sh_ref, sc_ref)

    h = h_ref[...]
    tm = h.shape[0]
    u = _gelu(_dot(h, wu_ref[...]))
    v = _gelu(_dot(h, wv_ref[...]))
    for hh in range(2):
        cols = slice(hh * HEAD_A, (hh + 1) * HEAD_A)
        vh = v[:, cols]
        vh = vh * _rms_scale(vh, HEAD_A) * gv_ref[...]
        if sample:
            va_ref[:, cols] = vh
            nb = tm // 8
            v3 = vh.reshape(nb, 8, HEAD_A)
            mixed = jnp.broadcast_to(bs_ref[:, cols][None], (nb, 8, HEAD_A))
            for s in range(8):
                row = jnp.broadcast_to(v3[:, s:s + 1, :], (nb, 8, HEAD_A))
                mixed = mixed + row * ws_ref[s, :, cols][None]
            a_ref[:, cols] = u[:, cols] * mixed.reshape(tm, HEAD_A)
        else:
            w = ws_ref[hh]
            vb = vh.astype(BF16)
            for c in range(tm // CHUNK):
                rows = slice(c * CHUNK, (c + 1) * CHUNK)
                mixed = _dot(w, vb[rows]) + bs_ref[hh]
                a_ref[rows, cols] = u[rows, cols] * mixed


def _uv_call(tok, x, g_norm, wu, wv, g_v, ws, bs):
    sample = tok.sample
    tm = tok.tile(512 if sample else 1024)
    if sample:
        ws_spec = pl.BlockSpec((8, 8, 2 * HEAD_A), lambda i, j: (0, 0, j))
        bs_spec = pl.BlockSpec((8, 2 * HEAD_A), lambda i, j: (0, j))
    else:
        assert tm % CHUNK == 0
        ws_spec = pl.BlockSpec((2, CHUNK, CHUNK), lambda i, j: (j, 0, 0))
        bs_spec = pl.BlockSpec((2, CHUNK, HEAD_A), lambda i, j: (j, 0, 0))
    blk = pl.BlockSpec((tm, 2 * HEAD_A), lambda i, j: (i, j))
    out_shape = [jax.ShapeDtypeStruct((tok.n_tok, D_A), F32)]
    out_specs = [blk]
    if sample:
        out_shape.append(jax.ShapeDtypeStruct((tok.n_tok, D_A), F32))
        out_specs.append(blk)
    return pl.pallas_call(
        functools.partial(_uv_kernel, sample=sample),
        grid=(tok.n_tok // tm, H_A // 2),
        in_specs=[
            _single((tm, D_MODEL), lambda i, j: (i, 0)),
            tok.mod_spec(tm, 3),
            tok.mod_spec(tm, 4),
            pl.BlockSpec((1, D_MODEL), lambda i, j: (0, 0)),
            pl.BlockSpec((D_MODEL, 2 * HEAD_A), lambda i, j: (0, j)),
            pl.BlockSpec((D_MODEL, 2 * HEAD_A), lambda i, j: (0, j)),
            pl.BlockSpec((1, HEAD_A), lambda i, j: (0, 0)),
            ws_spec,
            bs_spec,
        ],
        out_specs=out_specs,
        out_shape=out_shape,
        scratch_shapes=[pltpu.VMEM((tm, D_MODEL), BF16)],
        compiler_params=_params(("arbitrary", "arbitrary"), 48),
        name="uv",
    )(x, tok.mod, tok.mod, g_norm, wu, wv, g_v, ws, bs)


def _qkr_kernel(x_ref, sh_ref, sc_ref, g_ref, wb_ref, gq_ref, gkv_ref, tk_ref, cq_ref, ckv_ref, kr_ref,
                h_ref):
    _ada_norm_rows(h_ref, x_ref, g_ref, sh_ref, sc_ref)
    p = _dot(h_ref[...], wb_ref[...])
    cq = p[:, :Q_LORA]
    cq_ref[...] = (cq * _rms_scale(cq, Q_LORA) * gq_ref[...]).astype(BF16)
    ckv = p[:, Q_LORA:Q_LORA + KV_LORA]
    ckv_ref[...] = ckv * _rms_scale(ckv, KV_LORA) * gkv_ref[...]
    e = p[:, Q_LORA + KV_LORA:] * tk_ref[...]
    kr_ref[...] = e + pltpu.roll(e, 2 * HALF_ROPE, axis=1)


def _qkr_call(tok, x, g_norm, wb, g_q_a, g_kv_a, tk):
    tm = tok.tile(512)
    nb = wb.shape[1]
    row = lambda w: pl.BlockSpec((tm, w), lambda i: (i, 0))
    return pl.pallas_call(
        _qkr_kernel,
        grid=(tok.n_tok // tm,),
        in_specs=[
            row(D_MODEL),
            tok.mod_spec(tm, 3),
            tok.mod_spec(tm, 4),
            pl.BlockSpec((1, D_MODEL), lambda i: (0, 0)),
            _single((D_MODEL, nb), lambda i: (0, 0)),
            pl.BlockSpec((1, Q_LORA), lambda i: (0, 0)),
            pl.BlockSpec((1, KV_LORA), lambda i: (0, 0)),
            tok.pos_spec(tm, LANES),
        ],
        out_specs=[row(Q_LORA), row(KV_LORA), row(LANES)],
        out_shape=[
            jax.ShapeDtypeStruct((tok.n_tok, Q_LORA), BF16),
            jax.ShapeDtypeStruct((tok.n_tok, KV_LORA), F32),
            jax.ShapeDtypeStruct((tok.n_tok, LANES), F32),
        ],
        scratch_shapes=[pltpu.VMEM((tm, D_MODEL), BF16)],
        compiler_params=_params(("arbitrary",), 56),
        name="qkr",
    )(x, tok.mod, tok.mod, g_norm, wb, g_q_a, g_kv_a, tk)


def _q_heads(cq_ref, wqb_ref, tq_ref, ga_ref):
    q = _dot(cq_ref[...], wqb_ref[...])
    tq = tq_ref[...]
    ga = ga_ref[...]
    for h in range(H_M):
        a = q[:, h * QH_W:h * QH_W + NOPE_DIM]
        b = q[:, h * QH_W + NOPE_DIM:(h + 1) * QH_W]
        ss = jnp.sum(a * a + 0.5 * (b * b), axis=-1, keepdims=True)
        r = lax.rsqrt(ss * (1.0 / QK_DIM) + EPS) * (QK_DIM ** -0.5)
        yield h, a * ga * r, b * tq * r


def _qpost_p_kernel(cq_ref, ckv_ref, kr_ref, tq_ref, ga_ref, gkn_ref, gkr_ref, wqb_ref, wkvb_ref,
                    q_ref, k_ref, v_ref):
    for h, a, b in _q_heads(cq_ref, wqb_ref, tq_ref, ga_ref):
        q_ref[:, h * QH_W:h * QH_W + NOPE_DIM] = a.astype(BF16)
        q_ref[:, h * QH_W + NOPE_DIM:(h + 1) * QH_W] = b.astype(BF16)
    kv = _dot(ckv_ref[...].astype(BF16), wkvb_ref[...])
    v_ref[...] = kv[:, D_M:].astype(BF16)
    kr = kr_ref[...]
    rr = 0.5 * jnp.sum(kr * kr, axis=-1, keepdims=True)
    krg = kr * gkr_ref[...]
    for h in range(H_M):
        kn = kv[:, h * NOPE_DIM:(h + 1) * NOPE_DIM]
        r = lax.rsqrt((jnp.sum(kn * kn, axis=-1, keepdims=True) + rr) * (1.0 / QK_DIM) + EPS)
        k_ref[:, h * QH_W:h * QH_W + NOPE_DIM] = (kn * gkn_ref[...] * r).astype(BF16)
        k_ref[:, h * QH_W + NOPE_DIM:(h + 1) * QH_W] = (krg * r).astype(BF16)


def _qpost_s_kernel(cq_ref, tq_ref, ga_ref, wqb_ref, wkbt_ref, qa_ref):
    tm = cq_ref.shape[0]
    lane = lax.broadcasted_iota(jnp.int32, (tm, LANES), 1)
    for h, a, b in _q_heads(cq_ref, wqb_ref, tq_ref, ga_ref):
        qabs = _dot(a.astype(BF16), wkbt_ref[h])
        qr = jnp.where(lane < ROPE_DIM, b + pltpu.roll(b, ROPE_DIM, axis=1), 0.0)
        qa_ref[:, h, :, :KV_LORA] = qabs.reshape(tm // 8, 8, KV_LORA)
        qa_ref[:, h, :, KV_LORA:] = qr.reshape(tm // 8, 8, LANES)


def _qpost_call(tok, cq, ckv, kr, tq, ga, gkn, gkr, wqb, wkvb, wkbt):
    tm = tok.tile(256)
    row = lambda w: pl.BlockSpec((tm, w), lambda i: (i, 0))
    one = lambda w: pl.BlockSpec((1, w), lambda i: (0, 0))
    n = tok.n_tok
    if tok.sample:
        return pl.pallas_call(
            _qpost_s_kernel,
            grid=(n // tm,),
            in_specs=[
                row(Q_LORA), tok.pos_spec(tm, LANES), one(NOPE_DIM),
                _single((Q_LORA, H_M * QH_W), lambda i: (0, 0)),
                _single((H_M, NOPE_DIM, KV_LORA), lambda i: (0, 0, 0)),
            ],
            out_specs=pl.BlockSpec((tm // 8, H_M, 8, QA_W), lambda i: (i, 0, 0, 0)),
            out_shape=jax.ShapeDtypeStruct((n // 8, H_M, 8, QA_W), F32),
            compiler_params=_params(("arbitrary",), 56),
            name="qpost_s",
        )(cq, tq, ga, wqb, wkbt)
    return pl.pallas_call(
        _qpost_p_kernel,
        grid=(n // tm,),
        in_specs=[
            row(Q_LORA), row(KV_LORA), row(LANES), tok.pos_spec(tm, LANES),
            one(NOPE_DIM), one(NOPE_DIM), one(LANES),
            _single((Q_LORA, H_M * QH_W), lambda i: (0, 0)),
            _single((KV_LORA, 2 * D_M), lambda i: (0, 0)),
        ],
        out_specs=[row(H_M * QH_W), row(H_M * QH_W), row(D_M)],
        out_shape=[
            jax.ShapeDtypeStruct((n, H_M * QH_W), BF16),
            jax.ShapeDtypeStruct((n, H_M * QH_W), BF16),
            jax.ShapeDtypeStruct((n, D_M), BF16),
        ],
        compiler_params=_params(("arbitrary",), 56),
        name="qpost_p",
    )(cq, ckv, kr, tq, ga, gkn, gkr, wqb, wkvb)


def _attn_p_kernel(q_ref, k_ref, v_ref, o_ref):
    tq = q_ref.shape[0]
    qi = pl.program_id(2)
    for n in range(k_ref.shape[0] // tq):
        @pl.when(qi == n)
        def _(n=n):
            nk = (n + 1) * tq
            q_pos = n * tq + lax.broadcasted_iota(jnp.int32, (tq, nk), 0)
            k_pos = lax.broadcasted_iota(jnp.int32, (tq, nk), 1)
            for hh in range(q_ref.shape[1] // QH_W):
                qk_cols = slice(hh * QH_W, (hh + 1) * QH_W)
                v_cols = slice(hh * V_DIM, (hh + 1) * V_DIM)
                s = lax.dot_general(q_ref[:, qk_cols], k_ref[0:nk, qk_cols], _NT, preferred_element_type=F32)
                s = jnp.where(k_pos <= q_pos, s, -jnp.inf)
                e = jnp.exp(s - jnp.max(s, axis=-1, keepdims=True))
                p = e * (1.0 / jnp.sum(e, axis=-1, keepdims=True))
                o_ref[:, v_cols] = _dot(p.astype(BF16), v_ref[0:nk, v_cols])


def _attn_p_call(q, k, v, n_seq, seq_len):
    tq = min(512, seq_len)
    nq = seq_len // tq
    hs = ATTN_P_HEADS
    return pl.pallas_call(
        _attn_p_kernel,
        grid=(n_seq, H_M // hs, nq),
        in_specs=[
            pl.BlockSpec((tq, hs * QH_W), lambda b, h, i: (b * nq + i, h)),
            pl.BlockSpec((seq_len, hs * QH_W), lambda b, h, i: (b, h)),
            pl.BlockSpec((seq_len, hs * V_DIM), lambda b, h, i: (b, h)),
        ],
        out_specs=pl.BlockSpec((tq, hs * V_DIM), lambda b, h, i: (b * nq + i, h)),
        out_shape=jax.ShapeDtypeStruct((n_seq * seq_len, D_M), F32),
        compiler_params=_params(("arbitrary", "arbitrary", "arbitrary"), 48),
        name="attn_p",
    )(q, k, v)


def _attn_s_kernel(pt_ref, q_ref, cn_ref, rnt_ref, wkbt_ref, cache_c_ref, cache_r_ref, o_ref,
                   lhs_ref, qr_ref, *rest, npb, nblk):
    n_pg = npb * nblk
    slots = rest[:3 * nblk]
    cbs, rts, ss = slots[0::3], slots[1::3], slots[2::3]
    m_ref, l_ref, acc_ref, cpg_ref, rpg_ref, sem_ref = rest[3 * nblk:]
    j = pl.program_id(1)
    nq = q_ref.shape[1]
    ncol = H_M * nq
    nkd = H_M * NOPE_DIM
    mk = npb * PAGE_SIZE
    t = pl.program_id(0) * pl.num_programs(1) + j
    n_steps = pl.num_programs(0) * pl.num_programs(1)
    page_slot = t % 2

    def page_copies(step, slot):
        copies = []
        for i in range(n_pg):
            pid = pt_ref[step * n_pg + i]
            copies.append(pltpu.make_async_copy(cache_c_ref.at[0, pid], cpg_ref.at[slot, i], sem_ref.at[0, slot]))
            copies.append(pltpu.make_async_copy(cache_r_ref.at[0, pid], rpg_ref.at[slot, i], sem_ref.at[1, slot]))
        return copies

    @pl.when(t == 0)
    def _():
        for cp in page_copies(0, 0):
            cp.start()

    nxt = jnp.minimum(t + 1, n_steps - 1)
    for cp in page_copies(nxt, 1 - page_slot):
        cp.start()
    for cp in page_copies(t, page_slot):
        cp.wait()

    @pl.when(j == 0)
    def _():
        m_ref[...] = jnp.full_like(m_ref, -jnp.inf)
        l_ref[...] = jnp.zeros_like(l_ref)
        acc_ref[...] = jnp.zeros_like(acc_ref)
        q = q_ref[...].reshape(ncol, QA_W)
        lhs_ref[0:nkd, :] = wkbt_ref[...]
        lhs_ref[nkd:, :] = q[:, :KV_LORA].astype(BF16)
        qr_ref[...] = q[:, KV_LORA:].astype(BF16)
        for rt in rts:
            rt[ROPE_DIM:, :] = jnp.zeros((rt.shape[0] - ROPE_DIM, mk), BF16)

    def scores(cb, rt, s_out, n_keys, stage):
        chunk = min(KEY_CHUNK, n_keys)
        for n0 in range(0, n_keys, chunk):
            keys = slice(n0, n0 + chunk)
            rr = stage(n0, chunk)
            r = lax.dot_general(lhs_ref[...], cb[keys, :], _NT, preferred_element_type=F32)
            kn = r[:nkd].reshape(H_M, NOPE_DIM // 8, 8, chunk)
            tot = jnp.sum(jnp.sum(kn * kn, axis=1), axis=1, keepdims=True)
            s3 = (r[nkd:] + _dot(qr_ref[...], rt[:, keys])).reshape(H_M, nq, chunk)
            inv = lax.rsqrt((tot + rr[None]) * (1.0 / QK_DIM) + EPS)
            s_out[:, keys] = (s3 * inv).reshape(ncol, chunk)

    def fold(cb, s_in, n_keys, mask):
        s = s_in[:, 0:n_keys]
        if mask is not None:
            s = jnp.where(mask, s, -jnp.inf)
        m_old = m_ref[...]
        m_new = jnp.maximum(m_old, jnp.max(s, axis=1, keepdims=True))
        alpha = jnp.exp(m_old - m_new)
        p = jnp.exp(s - m_new)
        l_ref[...] = alpha * l_ref[...] + jnp.sum(p, axis=1, keepdims=True)
        m_ref[...] = m_new
        acc_ref[...] = alpha * acc_ref[...] + _dot(p.astype(BF16), cb[0:n_keys, :])

    def stage_pages(k):
        def stage(n0, n):
            rrs = []
            for i in range(n0 // PAGE_SIZE, (n0 + n) // PAGE_SIZE):
                keys = slice(i * PAGE_SIZE, (i + 1) * PAGE_SIZE)
                cbs[k][keys, :] = cpg_ref[page_slot, k * npb + i].astype(BF16)
                r = rpg_ref[page_slot, k * npb + i]
                rts[k][0:ROPE_DIM, keys] = r.astype(BF16)
                rrs.append(jnp.sum(r * r, axis=0, keepdims=True))
            return rrs[0] if len(rrs) == 1 else jnp.concatenate(rrs, axis=1)

        return stage

    scores(cbs[0], rts[0], ss[0], mk, stage_pages(0))
    for k in range(1, nblk):
        scores(cbs[k], rts[k], ss[k], mk, stage_pages(k))
        fold(cbs[k - 1], ss[k - 1], mk, None)
    fold(cbs[nblk - 1], ss[nblk - 1], mk, None)

    @pl.when(j == pl.num_programs(1) - 1)
    def _():
        def stage_new(n0, n):
            pad = jnp.zeros((n - nq, KV_LORA), F32)
            cbs[0][0:n, :] = jnp.concatenate([cn_ref[...], pad], axis=0).astype(BF16)
            rn = rnt_ref[...]
            rts[0][0:ROPE_DIM, 0:n] = rn.astype(BF16)
            return jnp.sum(rn * rn, axis=0, keepdims=True)

        scores(cbs[0], rts[0], ss[0], PAGE_SIZE, stage_new)
        key_t = lax.broadcasted_iota(jnp.int32, (ncol, PAGE_SIZE), 1)
        qq = lax.broadcasted_iota(jnp.int32, (ncol, PAGE_SIZE), 0) % nq
        fold(cbs[0], ss[0], PAGE_SIZE, key_t <= qq)
        o_ref[...] = acc_ref[...] * (1.0 / l_ref[...])

    @pl.when(t == n_steps - 1)
    def _():
        for cp in page_copies(nxt, 1 - page_slot):
            cp.wait()


def _attn_s_call(page_table, qa, ckv_new, krt_new, cache_ckv, cache_krope_t, wkbt):
    n_seq, n_pages = page_table.shape
    nq = qa.shape[2]
    assert nq == SEQ_ROWS
    nblk = max(d for d in range(1, ATTN_S_BLOCKS + 1) if n_pages % d == 0)
    npb = max(d for d in range(1, ATTN_S_PAGES + 1) if (n_pages // nblk) % d == 0)
    n_pg = npb * nblk
    ncol = H_M * nq
    nkd = H_M * NOPE_DIM
    mk = npb * PAGE_SIZE

    slot_scratch = [
        pltpu.VMEM((mk, KV_LORA), BF16),
        pltpu.VMEM((LANES, mk), BF16),
        pltpu.VMEM((ncol, mk), F32),
    ]
    grid_spec = pltpu.PrefetchScalarGridSpec(
        num_scalar_prefetch=1,
        grid=(n_seq, n_pages // n_pg),
        in_specs=[
            pl.BlockSpec((None, H_M, nq, QA_W), lambda b, j, pt: (b, 0, 0, 0)),
            pl.BlockSpec((None, nq, KV_LORA), lambda b, j, pt: (b, 0, 0)),
            pl.BlockSpec((None, ROPE_DIM, PAGE_SIZE), lambda b, j, pt: (b, 0, 0)),
            pl.BlockSpec((nkd, KV_LORA), lambda b, j, pt: (0, 0)),
            pl.BlockSpec(memory_space=pl.ANY),
            pl.BlockSpec(memory_space=pl.ANY),
        ],
        out_specs=pl.BlockSpec((None, ncol, KV_LORA), lambda b, j, pt: (b, 0, 0)),
        scratch_shapes=[
            pltpu.VMEM((nkd + ncol, KV_LORA), BF16),
            pltpu.VMEM((ncol, LANES), BF16),
        ] + slot_scratch * nblk + [
            pltpu.VMEM((ncol, 1), F32),
            pltpu.VMEM((ncol, 1), F32),
            pltpu.VMEM((ncol, KV_LORA), F32),
            pltpu.VMEM((2, n_pg, PAGE_SIZE, KV_LORA), F32),
            pltpu.VMEM((2, n_pg, ROPE_DIM, PAGE_SIZE), F32),
            pltpu.SemaphoreType.DMA((2, 2)),
        ],
    )
    return pl.pallas_call(
        functools.partial(_attn_s_kernel, npb=npb, nblk=nblk),
        grid_spec=grid_spec,
        out_shape=jax.ShapeDtypeStruct((n_seq, ncol, KV_LORA), F32),
        compiler_params=_params(("arbitrary", "arbitrary"), 56),
        name="attn_s",
    )(page_table.reshape(-1), qa, ckv_new, krt_new, wkbt, cache_ckv, cache_krope_t)


def _vexp_kernel(ctx_ref, wv_ref, o_ref):
    nb, nq, _ = ctx_ref.shape
    o_ref[...] = _dot(ctx_ref[...].reshape(nb * nq, KV_LORA).astype(BF16), wv_ref[...])


def _vexp_call(ctx, wvb):
    n_seq, _, nq, _ = ctx.shape
    return pl.pallas_call(
        _vexp_kernel,
        grid=(H_M,),
        in_specs=[
            pl.BlockSpec((n_seq, None, nq, KV_LORA), lambda h: (0, h, 0, 0)),
            pl.BlockSpec((None, KV_LORA, V_DIM), lambda h: (h, 0, 0)),
        ],
        out_specs=pl.BlockSpec((n_seq * nq, V_DIM), lambda h: (0, h)),
        out_shape=jax.ShapeDtypeStruct((n_seq * nq, D_M), F32),
        compiler_params=_params(("arbitrary",), 32),
        name="vexp",
    )(ctx, wvb)


def _mixout_kernel(a_ref, m_ref, ga_ref, gm_ref, x_ref, gt_ref, w_ref, o_ref, lhs_ref):
    @pl.when(pl.program_id(1) == 0)
    def _():
        def body(i, rows):
            a = a_ref[rows, :]
            lhs_ref[rows, :D_A] = (a * _rms_scale(a, D_A) * ga_ref[...]).astype(BF16)
            m = m_ref[rows, :]
            lhs_ref[rows, D_A:] = (m * _rms_scale(m, D_M) * gm_ref[...]).astype(BF16)

        _row_loop(a_ref.shape[0], body)

    y = _dot(lhs_ref[...], w_ref[...])
    if gt_ref.shape[0] == 1:
        o_ref[...] = x_ref[...] + gt_ref[...] * y
    else:
        o_ref[...] = y

        def body(i, rows):
            o_ref[rows, :] = x_ref[rows, :] + _mod_rows(gt_ref, i) * o_ref[rows, :]

        _row_loop(o_ref.shape[0], body)


def _mixout_call(tok, a, m, g_a, g_m, x, w_out):
    tm = tok.tile(512)
    tn = 512
    gt_spec = tok.mod_spec(tm, 5, width=tn, col=lambda i, j: j)
    return pl.pallas_call(
        _mixout_kernel,
        grid=(tok.n_tok // tm, D_MODEL // tn),
        in_specs=[
            pl.BlockSpec((tm, D_A), lambda i, j: (i, 0)),
            pl.BlockSpec((tm, D_M), lambda i, j: (i, 0)),
            pl.BlockSpec((1, D_A), lambda i, j: (0, 0)),
            pl.BlockSpec((1, D_M), lambda i, j: (0, 0)),
            pl.BlockSpec((tm, tn), lambda i, j: (i, j)),
            gt_spec,
            pl.BlockSpec((D_A + D_M, tn), lambda i, j: (0, j)),
        ],
        out_specs=pl.BlockSpec((tm, tn), lambda i, j: (i, j)),
        out_shape=jax.ShapeDtypeStruct((tok.n_tok, D_MODEL), F32),
        scratch_shapes=[pltpu.VMEM((tm, D_A + D_M), BF16)],
        compiler_params=_params(("arbitrary", "arbitrary"), 48),
        name="mixout",
    )(a, m, g_a, g_m, x, tok.mod, w_out)


def _rope_table(pos):
    inv_freq = ROPE_THETA ** (-jnp.arange(HALF_ROPE, dtype=F32) / HALF_ROPE)
    ang = pos.astype(F32)[:, None] * inv_freq[None, :]
    cos, sin = jnp.cos(ang), jnp.sin(ang)
    return jnp.concatenate([cos, cos, sin, sin], axis=-1)


def _dup_rope_gain(g):
    g1, g2 = g[NOPE_DIM:NOPE_DIM + HALF_ROPE], g[NOPE_DIM + HALF_ROPE:]
    return jnp.concatenate([g1, g2, g1, g2])[None, :]


def _layer(tok, x, pos, p, attn_fn):
    x = _ffn_call(tok, x, 0, p["g_norm1"], p["wg1"], p["wu1"], p["wd1"])
    uv = _uv_call(tok, x, p["g_norm2"], p["wu"], p["wv"], p["g_v_a"],
                  p["ws_s"] if tok.sample else p["ws_p"], p["bs_s"] if tok.sample else p["bs_p"])
    a_out = uv[0]
    tk = _rope_table(pos)
    cq, ckv, krd = _qkr_call(tok, x, p["g_norm2"], p["wb"], p["g_q_a"], p["g_kv_a"], tk)
    m_out = attn_fn(cq, ckv, krd, tk)
    x = _mixout_call(tok, a_out, m_out, p["g_out_a"], p["g_out_m"], x, p["w_out"])
    x = _ffn_call(tok, x, 6, p["g_norm3"], p["wg2"], p["wu2"], p["wd2"])
    return x, ckv, krd[:, :ROPE_DIM], (uv[1] if tok.sample else None)


def kernel(x_prompt, x_sample, cache_ckv, cache_krope, page_table, c_prompt, c_sample, w_ada, b_ada, g_norm1, w_ff1_gate, w_ff1_up, w_ff1_down, g_norm2, w_in, w_s, b_s, g_v_a, g_q_a, w_q_b, g_kv_a, w_kv_b, g_q_norm, g_k_norm, g_out_a, g_out_m, w_out, g_norm3, w_ff2_gate, w_ff2_up, w_ff2_down):
    n_seq_p, seq_len, _ = x_prompt.shape
    n_seq_s, dec_len, _ = x_sample.shape
    n_pages = page_table.shape[1]
    past = n_pages * PAGE_SIZE
    depth = w_ada.shape[0]
    assert depth == 1 and dec_len == 8

    w_in0 = w_in[0]
    o = 2 * D_A
    w_kr = w_in0[:, o + Q_LORA + KV_LORA:]
    w1, w2 = w_kr[:, :HALF_ROPE], w_kr[:, HALF_ROPE:]
    wq = w_q_b[0]
    qn, q1, q2 = wq[..., :NOPE_DIM], wq[..., NOPE_DIM:NOPE_DIM + HALF_ROPE], wq[..., NOPE_DIM + HALF_ROPE:]
    wkv = w_kv_b[0]
    wk_n, wv_n = wkv[..., :NOPE_DIM], wkv[..., NOPE_DIM:]
    gq, gk = g_q_norm[0], g_k_norm[0]
    tril = jnp.tril(jnp.ones((CHUNK, CHUNK), dtype=bool))
    ws_t = jnp.where(tril[None], w_s[0], 0.0)
    p = {
        "g_norm1": g_norm1, "g_norm2": g_norm2, "g_norm3": g_norm3,
        "wg1": w_ff1_gate[0].astype(BF16), "wu1": w_ff1_up[0].astype(BF16), "wd1": w_ff1_down[0].astype(BF16),
        "wg2": w_ff2_gate[0].astype(BF16), "wu2": w_ff2_up[0].astype(BF16), "wd2": w_ff2_down[0].astype(BF16),
        "wu": w_in0[:, :D_A].astype(BF16), "wv": w_in0[:, D_A:o].astype(BF16),
        "wb": jnp.concatenate([w_in0[:, o:o + Q_LORA + KV_LORA], w1, w2, -w2, w1], axis=1).astype(BF16),
        "g_v_a": g_v_a, "g_q_a": g_q_a, "g_kv_a": g_kv_a,
        "ws_p": ws_t.astype(BF16),
        "bs_p": jnp.broadcast_to(b_s[0][:, :, None], (H_A, CHUNK, HEAD_A)),
        "ws_s": jnp.repeat(ws_t[:, :dec_len, :dec_len].transpose(2, 1, 0), HEAD_A, axis=2),
        "bs_s": jnp.repeat(b_s[0][:, :dec_len].T, HEAD_A, axis=1),
        "g_out_a": g_out_a, "g_out_m": g_out_m, "w_out": w_out[0].astype(BF16),
    }
    wqb = jnp.concatenate([qn, q1, q2, -q2, q1], axis=-1).reshape(Q_LORA, H_M * QH_W).astype(BF16)
    wkvb = jnp.concatenate([wk_n.reshape(KV_LORA, -1), wv_n.reshape(KV_LORA, -1)], axis=1).astype(BF16)
    wkbt =wk_n.transpose(1, 2, 0).astype(BF16)
    wvb = wv_n.transpose(1, 0, 2).astype(BF16)
    gq_dup, gk_dup = _dup_rope_gain(gq), _dup_rope_gain(gk)
    gq_n, gk_n = gq[None, :NOPE_DIM], gk[None, :NOPE_DIM]

    n_c = n_seq_p + n_seq_s
    pad = (-n_c) % 8
    c_all = jnp.concatenate([c_prompt, c_sample, jnp.zeros((pad, D_MODEL), F32)], axis=0)
    mod = _mod_call(c_all, w_ada[0], b_ada)
    mod_p = mod[:n_seq_p].reshape(n_seq_p * N_MOD, 1, D_MODEL)
    mod_s = mod[n_seq_p:n_c]

    tok_p = _Tokens(False, n_seq_p * seq_len, seq_len, mod_p)

    def attn_prompt(cq, ckv, krd, tk):
        q, k, v = _qpost_call(tok_p, cq, ckv, krd, tk * gq_dup, gq_n, gk_n, gk_dup, wqb, wkvb, None)
        return _attn_p_call(q, k, v, n_seq_p, seq_len)

    yp, ckv_p, kr_p, _ = _layer(tok_p, x_prompt.reshape(-1, D_MODEL), jnp.arange(seq_len), p, attn_prompt)

    tok_s = _Tokens(True, n_seq_s * dec_len, dec_len, mod_s)
    cache_krope_t = jnp.swapaxes(cache_krope, 2, 3)

    def attn_sample(cq, ckv, krd, tk):
        qa = _qpost_call(tok_s, cq, None, None, tk * (gq_dup * gk_dup), gq_n * gk_n, None, None, wqb, None, wkbt)
        krt_new = jnp.swapaxes(krd[:, :ROPE_DIM].reshape(n_seq_s, dec_len, ROPE_DIM), 1, 2)
        krt_new = jnp.pad(krt_new, ((0, 0), (0, 0), (0, PAGE_SIZE - dec_len)))
        ctx = _attn_s_call(page_table, qa, ckv.reshape(n_seq_s, dec_len, KV_LORA), krt_new,
                           cache_ckv, cache_krope_t, wkbt.reshape(H_M * NOPE_DIM, KV_LORA))
        return _vexp_call(ctx.reshape(n_seq_s, H_M, dec_len, KV_LORA), wvb)

    pos_s = jnp.tile(past + jnp.arange(dec_len), n_seq_s)
    ys, ckv_s, kr_s, va_s = _layer(tok_s, x_sample.reshape(-1, D_MODEL), pos_s, p, attn_sample)

    return (
        yp.reshape(n_seq_p, seq_len, D_MODEL),
        ys.reshape(n_seq_s, dec_len, D_MODEL),
        ckv_p.reshape(1, n_seq_p, seq_len, KV_LORA),
        kr_p.reshape(1, n_seq_p, seq_len, ROPE_DIM),
        ckv_s.reshape(1, n_seq_s, dec_len, KV_LORA),
        kr_s.reshape(1, n_seq_s, dec_len, ROPE_DIM),
        va_s.reshape(1, n_seq_s, dec_len, H_A, HEAD_A),
    )
```

```python
import functools

import jax
import jax.numpy as jnp
from jax import lax
from jax.experimental import pallas as pl
from jax.experimental.pallas import tpu as pltpu

F32 = jnp.float32
BF16 = jnp.bfloat16

D_MODEL = 4096
D_A = 2048
HEAD_A = 128
H_A = 16
CHUNK = 128
D_M = 2048
V_DIM = 128
H_M = 16
NOPE_DIM = 128
ROPE_DIM = 64
HALF_ROPE = ROPE_DIM // 2
QK_DIM = NOPE_DIM + ROPE_DIM
Q_LORA = 1024
KV_LORA = 512
ROPE_THETA = 10000.0
D_FF = 11008
N_MOD = 9
PAGE_SIZE = 128
EPS = 1e-6

QH_W = 256
QA_W = KV_LORA + 128
LANES = 128
MIB = 1024 * 1024
ROW_STEP = 16
ROW_UNROLL = 4
SEQ_ROWS = 8
KEY_CHUNK = 256
ATTN_P_HEADS = 4
ATTN_S_PAGES = 8
ATTN_S_BLOCKS = 4
FFN_DOWN_COLS = 512

_NT = (((1,), (1,)), ((), ()))
_TN = (((0,), (0,)), ((), ()))


def _dot(a, b):
    return jnp.dot(a, b, preferred_element_type=F32)


def _params(semantics, vmem_mib):
    return pltpu.CompilerParams(dimension_semantics=semantics, vmem_limit_bytes=vmem_mib * MIB)


def _single(block_shape, index_map):
    return pl.BlockSpec(block_shape, index_map, pipeline_mode=pl.Buffered(1))


def _ada_norm(x, g, sh, sc):
    r = lax.rsqrt(jnp.mean(x * x, axis=-1, keepdims=True) + EPS)
    return (x * r * g) * (1.0 + sc) + sh


def _mod_rows(ref, i):
    if ref.shape[0] == 1:
        return ref[...]
    per = ROW_STEP // SEQ_ROWS
    parts = [jnp.broadcast_to(ref[pl.ds(i * per + k, 1), :], (SEQ_ROWS, ref.shape[1])) for k in range(per)]
    return jnp.concatenate(parts, axis=0)


def _row_loop(n_rows, body):
    def step(i, carry):
        body(i, pl.ds(pl.multiple_of(i * ROW_STEP, ROW_STEP), ROW_STEP))
        return carry

    lax.fori_loop(0, n_rows // ROW_STEP, step, 0, unroll=ROW_UNROLL)


def _ada_norm_rows(h_ref, x_ref, g_ref, sh_ref, sc_ref):
    if sh_ref.shape[0] == 1:
        gain = g_ref[...] * (1.0 + sc_ref[...])

        def body(i, rows):
            x = x_ref[rows, :]
            r = lax.rsqrt(jnp.mean(x * x, axis=-1, keepdims=True) + EPS)
            h_ref[rows, :] = ((x * r) * gain + sh_ref[...]).astype(BF16)
    else:
        def body(i, rows):
            h_ref[rows, :] = _ada_norm(x_ref[rows, :], g_ref[...], _mod_rows(sh_ref, i),
                                       _mod_rows(sc_ref, i)).astype(BF16)

    _row_loop(x_ref.shape[0], body)


def _rms_scale(x, width):
    return lax.rsqrt(jnp.sum(x * x, axis=-1, keepdims=True) * (1.0 / width) + EPS)


def _gelu(x):
    return jax.nn.gelu(x)


class _Tokens:
    def __init__(self, sample, n_tok, seq_len, mod):
        self.sample = sample
        self.n_tok = n_tok
        self.seq_len = seq_len
        self.mod = mod

    def mod_spec(self, tm, k, width=D_MODEL, col=None):
        per_row = D_MODEL // width
        col = col or (lambda *_: 0)
        if self.sample:
            return pl.BlockSpec((tm // SEQ_ROWS, width), lambda i, *r: (i, k * per_row + col(i, *r)))
        per_seq = self.seq_len // tm
        return pl.BlockSpec((None, 1, width), lambda i, *r: ((i // per_seq) * N_MOD + k, 0, col(i, *r)))

    def pos_spec(self, tm, width):
        if self.sample:
            return pl.BlockSpec((tm, width), lambda i, *_: (i, 0))
        per_seq = self.seq_len // tm
        return pl.BlockSpec((tm, width), lambda i, *_: (i % per_seq, 0))

    def tile(self, want):
        tm = min(want, self.n_tok if self.sample else self.seq_len)
        assert self.n_tok % tm == 0 and (self.sample or self.seq_len % tm == 0)
        return tm


def _mod_kernel(c_ref, w_ref, b_ref, o_ref):
    c = c_ref[...]
    a = (c * jax.nn.sigmoid(c)).astype(BF16)
    o_ref[...] = _dot(a, w_ref[...].astype(BF16)) + b_ref[...]


def _mod_call(c, w_ada, b_ada):
    m, n = c.shape[0], w_ada.shape[1]
    tn = 1024
    return pl.pallas_call(
        _mod_kernel,
        grid=(n // tn,),
        in_specs=[
            pl.BlockSpec((m, D_MODEL), lambda j: (0, 0)),
            pl.BlockSpec((D_MODEL, tn), lambda j: (0, j)),
            pl.BlockSpec((1, tn), lambda j: (0, j)),
        ],
        out_specs=pl.BlockSpec((m, tn), lambda j: (0, j)),
        out_shape=jax.ShapeDtypeStruct((m, n), F32),
        compiler_params=_params(("arbitrary",), 56),
        name="mod",
    )(c, w_ada, b_ada)


def _ffn_kernel(x_ref, sh_ref, sc_ref, gt_ref, g_ref, wg_ref, wu_ref, wd_ref, o_ref, h_ref):
    k = pl.program_id(1)

    @pl.when(k == 0)
    def _():
        _ada_norm_rows(h_ref, x_ref, g_ref, sh_ref, sc_ref)
        o_ref[...] = jnp.zeros_like(o_ref)

    h = h_ref[...]
    g = _dot(h, wg_ref[...])
    u = _dot(h, wu_ref[...])
    a = (g * jax.nn.sigmoid(g) * u).astype(BF16)
    for n0 in range(0, D_MODEL, FFN_DOWN_COLS):
        cols = slice(n0, n0 + FFN_DOWN_COLS)
        o_ref[:, cols] += _dot(a, wd_ref[:, cols])

    @pl.when(k == pl.num_programs(1) - 1)
    def _():
        def body(i, rows):
            o_ref[rows, :] = x_ref[rows, :] + (0.5 * _mod_rows(gt_ref, i)) * o_ref[rows, :]

        _row_loop(o_ref.shape[0], body)


def _ffn_call(tok, x, mod_base, g_norm, wg, wu, wd):
    tm = tok.tile(512 if tok.sample else 1024)
    tf = 256
    n_ff = wg.shape[1]
    return pl.pallas_call(
        _ffn_kernel,
        grid=(tok.n_tok // tm, n_ff // tf),
        in_specs=[
            _single((tm, D_MODEL), lambda i, k: (i, 0)),
            tok.mod_spec(tm, mod_base),
            tok.mod_spec(tm, mod_base + 1),
            tok.mod_spec(tm, mod_base + 2),
            pl.BlockSpec((1, D_MODEL), lambda i, k: (0, 0)),
            pl.BlockSpec((D_MODEL, tf), lambda i, k: (0, k)),
            pl.BlockSpec((D_MODEL, tf), lambda i, k: (0, k)),
            pl.BlockSpec((tf, D_MODEL), lambda i, k: (k, 0)),
        ],
        out_specs=_single((tm, D_MODEL), lambda i, k: (i, 0)),
        out_shape=jax.ShapeDtypeStruct((tok.n_tok, D_MODEL), F32),
        scratch_shapes=[pltpu.VMEM((tm, D_MODEL), BF16)],
        compiler_params=_params(("arbitrary", "arbitrary"), 60),
        name="ffn",
    )(x, tok.mod, tok.mod, tok.mod, g_norm, wg, wu, wd)


def _uv_kernel(x_ref, sh_ref, sc_ref, g_ref, wu_ref, wv_ref, gv_ref, ws_ref, bs_ref, *rest, sample):
    if sample:
        a_ref, va_ref, h_ref = rest
    else:
        a_ref, h_ref = rest
    j = pl.program_id(1)

    @pl.when(j == 0)
    def _():
        _ada_norm_rows(h_ref, x_ref, g_ref, sh_ref, sc_ref)

    h = h_ref[...]
    tm = h.shape[0]
    u = _gelu(_dot(h, wu_ref[...]))
    v = _gelu(_dot(h, wv_ref[...]))
    for hh in range(2):
        cols = slice(hh * HEAD_A, (hh + 1) * HEAD_A)
        vh = v[:, cols]
        vh = vh * _rms_scale(vh, HEAD_A) * gv_ref[...]
        if sample:
            va_ref[:, cols] = vh
            nb = tm // 8
            v3 = vh.reshape(nb, 8, HEAD_A)
            mixed = jnp.broadcast_to(bs_ref[:, cols][None], (nb, 8, HEAD_A))
            for s in range(8):
                row = jnp.broadcast_to(v3[:, s:s + 1, :], (nb, 8, HEAD_A))
                mixed = mixed + row * ws_ref[s, :, cols][None]
            a_ref[:, cols] = u[:, cols] * mixed.reshape(tm, HEAD_A)
        else:
            w = ws_ref[hh]
            vb = vh.astype(BF16)
            for c in range(tm // CHUNK):
                rows = slice(c * CHUNK, (c + 1) * CHUNK)
                mixed = _dot(w, vb[rows]) + bs_ref[hh]
                a_ref[rows, cols] = u[rows, cols] * mixed


def _uv_call(tok, x, g_norm, wu, wv, g_v, ws, bs):
    sample = tok.sample
    tm = tok.tile(512 if sample else 1024)
    if sample:
        ws_spec = pl.BlockSpec((8, 8, 2 * HEAD_A), lambda i, j: (0, 0, j))
        bs_spec = pl.BlockSpec((8, 2 * HEAD_A), lambda i, j: (0, j))
    else:
        assert tm % CHUNK == 0
        ws_spec = pl.BlockSpec((2, CHUNK, CHUNK), lambda i, j: (j, 0, 0))
        bs_spec = pl.BlockSpec((2, CHUNK, HEAD_A), lambda i, j: (j, 0, 0))
    blk = pl.BlockSpec((tm, 2 * HEAD_A), lambda i, j: (i, j))
    out_shape = [jax.ShapeDtypeStruct((tok.n_tok, D_A), F32)]
    out_specs = [blk]
    if sample:
        out_shape.append(jax.ShapeDtypeStruct((tok.n_tok, D_A), F32))
        out_specs.append(blk)
    return pl.pallas_call(
        functools.partial(_uv_kernel, sample=sample),
        grid=(tok.n_tok // tm, H_A // 2),
        in_specs=[
            _single((tm, D_MODEL), lambda i, j: (i, 0)),
            tok.mod_spec(tm, 3),
            tok.mod_spec(tm, 4),
            pl.BlockSpec((1, D_MODEL), lambda i, j: (0, 0)),
            pl.BlockSpec((D_MODEL, 2 * HEAD_A), lambda i, j: (0, j)),
            pl.BlockSpec((D_MODEL, 2 * HEAD_A), lambda i, j: (0, j)),
            pl.BlockSpec((1, HEAD_A), lambda i, j: (0, 0)),
            ws_spec,
            bs_spec,
        ],
        out_specs=out_specs,
        out_shape=out_shape,
        scratch_shapes=[pltpu.VMEM((tm, D_MODEL), BF16)],
        compiler_params=_params(("arbitrary", "arbitrary"), 48),
        name="uv",
    )(x, tok.mod, tok.mod, g_norm, wu, wv, g_v, ws, bs)


def _qkr_kernel(x_ref, sh_ref, sc_ref, g_ref, wb_ref, gq_ref, gkv_ref, tk_ref, cq_ref, ckv_ref, kr_ref,
                h_ref):
    _ada_norm_rows(h_ref, x_ref, g_ref, sh_ref, sc_ref)
    p = _dot(h_ref[...], wb_ref[...])
    cq = p[:, :Q_LORA]
    cq_ref[...] = (cq * _rms_scale(cq, Q_LORA) * gq_ref[...]).astype(BF16)
    ckv = p[:, Q_LORA:Q_LORA + KV_LORA]
    ckv_ref[...] = ckv * _rms_scale(ckv, KV_LORA) * gkv_ref[...]
    e = p[:, Q_LORA + KV_LORA:] * tk_ref[...]
    kr_ref[...] = e + pltpu.roll(e, 2 * HALF_ROPE, axis=1)


def _qkr_call(tok, x, g_norm, wb, g_q_a, g_kv_a, tk):
    tm = tok.tile(512)
    nb = wb.shape[1]
    row = lambda w: pl.BlockSpec((tm, w), lambda i: (i, 0))
    return pl.pallas_call(
        _qkr_kernel,
        grid=(tok.n_tok // tm,),
        in_specs=[
            row(D_MODEL),
            tok.mod_spec(tm, 3),
            tok.mod_spec(tm, 4),
            pl.BlockSpec((1, D_MODEL), lambda i: (0, 0)),
            _single((D_MODEL, nb), lambda i: (0, 0)),
            pl.BlockSpec((1, Q_LORA), lambda i: (0, 0)),
            pl.BlockSpec((1, KV_LORA), lambda i: (0, 0)),
            tok.pos_spec(tm, LANES),
        ],
        out_specs=[row(Q_LORA), row(KV_LORA), row(LANES)],
        out_shape=[
            jax.ShapeDtypeStruct((tok.n_tok, Q_LORA), BF16),
            jax.ShapeDtypeStruct((tok.n_tok, KV_LORA), F32),
            jax.ShapeDtypeStruct((tok.n_tok, LANES), F32),
        ],
        scratch_shapes=[pltpu.VMEM((tm, D_MODEL), BF16)],
        compiler_params=_params(("arbitrary",), 56),
        name="qkr",
    )(x, tok.mod, tok.mod, g_norm, wb, g_q_a, g_kv_a, tk)


def _q_heads(cq_ref, wqb_ref, tq_ref, ga_ref):
    q = _dot(cq_ref[...], wqb_ref[...])
    tq = tq_ref[...]
    ga = ga_ref[...]
    for h in range(H_M):
        a = q[:, h * QH_W:h * QH_W + NOPE_DIM]
        b = q[:, h * QH_W + NOPE_DIM:(h + 1) * QH_W]
        ss = jnp.sum(a * a + 0.5 * (b * b), axis=-1, keepdims=True)
        r = lax.rsqrt(ss * (1.0 / QK_DIM) + EPS) * (QK_DIM ** -0.5)
        yield h, a * ga * r, b * tq * r


def _qpost_p_kernel(cq_ref, ckv_ref, kr_ref, tq_ref, ga_ref, gkn_ref, gkr_ref, wqb_ref, wkvb_ref,
                    q_ref, k_ref, v_ref):
    for h, a, b in _q_heads(cq_ref, wqb_ref, tq_ref, ga_ref):
        q_ref[:, h * QH_W:h * QH_W + NOPE_DIM] = a.astype(BF16)
        q_ref[:, h * QH_W + NOPE_DIM:(h + 1) * QH_W] = b.astype(BF16)
    kv = _dot(ckv_ref[...].astype(BF16), wkvb_ref[...])
    v_ref[...] = kv[:, D_M:].astype(BF16)
    kr = kr_ref[...]
    rr = 0.5 * jnp.sum(kr * kr, axis=-1, keepdims=True)
    krg = kr * gkr_ref[...]
    for h in range(H_M):
        kn = kv[:, h * NOPE_DIM:(h + 1) * NOPE_DIM]
        r = lax.rsqrt((jnp.sum(kn * kn, axis=-1, keepdims=True) + rr) * (1.0 / QK_DIM) + EPS)
        k_ref[:, h * QH_W:h * QH_W + NOPE_DIM] = (kn * gkn_ref[...] * r).astype(BF16)
        k_ref[:, h * QH_W + NOPE_DIM:(h + 1) * QH_W] = (krg * r).astype(BF16)


def _qpost_s_kernel(cq_ref, tq_ref, ga_ref, wqb_ref, wkbt_ref, qa_ref):
    tm = cq_ref.shape[0]
    lane = lax.broadcasted_iota(jnp.int32, (tm, LANES), 1)
    for h, a, b in _q_heads(cq_ref, wqb_ref, tq_ref, ga_ref):
        qabs = _dot(a.astype(BF16), wkbt_ref[h])
        qr = jnp.where(lane < ROPE_DIM, b + pltpu.roll(b, ROPE_DIM, axis=1), 0.0)
        qa_ref[:, h, :, :KV_LORA] = qabs.reshape(tm // 8, 8, KV_LORA)
        qa_ref[:, h, :, KV_LORA:] = qr.reshape(tm // 8, 8, LANES)


def _qpost_call(tok, cq, ckv, kr, tq, ga, gkn, gkr, wqb, wkvb, wkbt):
    tm = tok.tile(256)
    row = lambda w: pl.BlockSpec((tm, w), lambda i: (i, 0))
    one = lambda w: pl.BlockSpec((1, w), lambda i: (0, 0))
    n = tok.n_tok
    if tok.sample:
        return pl.pallas_call(
            _qpost_s_kernel,
            grid=(n // tm,),
            in_specs=[
                row(Q_LORA), tok.pos_spec(tm, LANES), one(NOPE_DIM),
                _single((Q_LORA, H_M * QH_W), lambda i: (0, 0)),
                _single((H_M, NOPE_DIM, KV_LORA), lambda i: (0, 0, 0)),
            ],
            out_specs=pl.BlockSpec((tm // 8, H_M, 8, QA_W), lambda i: (i, 0, 0, 0)),
            out_shape=jax.ShapeDtypeStruct((n // 8, H_M, 8, QA_W), F32),
            compiler_params=_params(("arbitrary",), 56),
            name="qpost_s",
        )(cq, tq, ga, wqb, wkbt)
    return pl.pallas_call(
        _qpost_p_kernel,
        grid=(n // tm,),
        in_specs=[
            row(Q_LORA), row(KV_LORA), row(LANES), tok.pos_spec(tm, LANES),
            one(NOPE_DIM), one(NOPE_DIM), one(LANES),
            _single((Q_LORA, H_M * QH_W), lambda i: (0, 0)),
            _single((KV_LORA, 2 * D_M), lambda i: (0, 0)),
        ],
        out_specs=[row(H_M * QH_W), row(H_M * QH_W), row(D_M)],
        out_shape=[
            jax.ShapeDtypeStruct((n, H_M * QH_W), BF16),
            jax.ShapeDtypeStruct((n, H_M * QH_W), BF16),
            jax.ShapeDtypeStruct((n, D_M), BF16),
        ],
        compiler_params=_params(("arbitrary",), 56),
        name="qpost_p",
    )(cq, ckv, kr, tq, ga, gkn, gkr, wqb, wkvb)


def _attn_p_kernel(q_ref, k_ref, v_ref, o_ref):
    tq = q_ref.shape[0]
    qi = pl.program_id(2)
    for n in range(k_ref.shape[0] // tq):
        @pl.when(qi == n)
        def _(n=n):
            nk = (n + 1) * tq
            q_pos = n * tq + lax.broadcasted_iota(jnp.int32, (tq, nk), 0)
            k_pos = lax.broadcasted_iota(jnp.int32, (tq, nk), 1)
            for hh in range(q_ref.shape[1] // QH_W):
                qk_cols = slice(hh * QH_W, (hh + 1) * QH_W)
                v_cols = slice(hh * V_DIM, (hh + 1) * V_DIM)
                s = lax.dot_general(q_ref[:, qk_cols], k_ref[0:nk, qk_cols], _NT, preferred_element_type=F32)
                s = jnp.where(k_pos <= q_pos, s, -jnp.inf)
                e = jnp.exp(s - jnp.max(s, axis=-1, keepdims=True))
                p = e * (1.0 / jnp.sum(e, axis=-1, keepdims=True))
                o_ref[:, v_cols] = _dot(p.astype(BF16), v_ref[0:nk, v_cols])


def _attn_p_call(q, k, v, n_seq, seq_len):
    tq = min(512, seq_len)
    nq = seq_len // tq
    hs = ATTN_P_HEADS
    return pl.pallas_call(
        _attn_p_kernel,
        grid=(n_seq, H_M // hs, nq),
        in_specs=[
            pl.BlockSpec((tq, hs * QH_W), lambda b, h, i: (b * nq + i, h)),
            pl.BlockSpec((seq_len, hs * QH_W), lambda b, h, i: (b, h)),
            pl.BlockSpec((seq_len, hs * V_DIM), lambda b, h, i: (b, h)),
        ],
        out_specs=pl.BlockSpec((tq, hs * V_DIM), lambda b, h, i: (b * nq + i, h)),
        out_shape=jax.ShapeDtypeStruct((n_seq * seq_len, D_M), F32),
        compiler_params=_params(("arbitrary", "arbitrary", "arbitrary"), 48),
        name="attn_p",
    )(q, k, v)


def _attn_s_kernel(pt_ref, q_ref, cn_ref, rnt_ref, wkbt_ref, cache_c_ref, cache_r_ref, o_ref,
                   lhs_ref, qr_ref, *rest, npb, nblk):
    n_pg = npb * nblk
    slots = rest[:3 * nblk]
    cbs, rts, ss = slots[0::3], slots[1::3], slots[2::3]
    m_ref, l_ref, acc_ref, cpg_ref, rpg_ref, sem_ref = rest[3 * nblk:]
    j = pl.program_id(1)
    nq = q_ref.shape[1]
    ncol = H_M * nq
    nkd = H_M * NOPE_DIM
    mk = npb * PAGE_SIZE
    t = pl.program_id(0) * pl.num_programs(1) + j
    n_steps = pl.num_programs(0) * pl.num_programs(1)
    page_slot = t % 2

    def page_copies(step, slot):
        copies = []
        for i in range(n_pg):
            pid = pt_ref[step * n_pg + i]
            copies.append(pltpu.make_async_copy(cache_c_ref.at[0, pid], cpg_ref.at[slot, i], sem_ref.at[0, slot]))
            copies.append(pltpu.make_async_copy(cache_r_ref.at[0, pid], rpg_ref.at[slot, i], sem_ref.at[1, slot]))
        return copies

    @pl.when(t == 0)
    def _():
        for cp in page_copies(0, 0):
            cp.start()
        lhs_ref[0:nkd, :] = wkbt_ref[...]

    nxt = jnp.minimum(t + 1, n_steps - 1)
    for cp in page_copies(nxt, 1 - page_slot):
        cp.start()
    for cp in page_copies(t, page_slot):
        cp.wait()

    @pl.when(j == 0)
    def _():
        m_ref[...] = jnp.full_like(m_ref, -jnp.inf)
        l_ref[...] = jnp.zeros_like(l_ref)
        acc_ref[...] = jnp.zeros_like(acc_ref)
        q = q_ref[...].reshape(ncol, QA_W)
        lhs_ref[nkd:, :] = q[:, :KV_LORA].astype(BF16)
        qr_ref[...] = q[:, KV_LORA:].astype(BF16)
        for rt in rts:
            rt[ROPE_DIM:, :] = jnp.zeros((rt.shape[0] - ROPE_DIM, mk), BF16)

    def scores(cb, rt, s_out, n_keys, stage):
        chunk = min(KEY_CHUNK, n_keys)
        for n0 in range(0, n_keys, chunk):
            keys = slice(n0, n0 + chunk)
            rr = stage(n0, chunk)
            r = lax.dot_general(lhs_ref[...], cb[keys, :], _NT, preferred_element_type=F32)
            kn = r[:nkd].reshape(H_M, NOPE_DIM // 8, 8, chunk)
            tot = jnp.sum(jnp.sum(kn * kn, axis=1), axis=1, keepdims=True)
            s3 = (r[nkd:] + _dot(qr_ref[...], rt[:, keys])).reshape(H_M, nq, chunk)
            inv = lax.rsqrt((tot + rr[None]) * (1.0 / QK_DIM) + EPS)
            s_out[:, keys] = (s3 * inv).reshape(ncol, chunk)

    def fold(cb, s_in, n_keys, mask):
        s = s_in[:, 0:n_keys]
        if mask is not None:
            s = jnp.where(mask, s, -jnp.inf)
        m_old = m_ref[...]
        m_new = jnp.maximum(m_old, jnp.max(s, axis=1, keepdims=True))
        alpha = jnp.exp(m_old - m_new)
        p = jnp.exp(s - m_new)
        l_ref[...] = alpha * l_ref[...] + jnp.sum(p, axis=1, keepdims=True)
        m_ref[...] = m_new
        acc_ref[...] = alpha * acc_ref[...] + _dot(p.astype(BF16), cb[0:n_keys, :])

    def stage_pages(k):
        def stage(n0, n):
            rrs = []
            for i in range(n0 // PAGE_SIZE, (n0 + n) // PAGE_SIZE):
                keys = slice(i * PAGE_SIZE, (i + 1) * PAGE_SIZE)
                cbs[k][keys, :] = cpg_ref[page_slot, k * npb + i].astype(BF16)
                r = rpg_ref[page_slot, k * npb + i]
                rts[k][0:ROPE_DIM, keys] = r.astype(BF16)
                rrs.append(jnp.sum(r * r, axis=0, keepdims=True))
            return rrs[0] if len(rrs) == 1 else jnp.concatenate(rrs, axis=1)

        return stage

    scores(cbs[0], rts[0], ss[0], mk, stage_pages(0))
    for k in range(1, nblk):
        scores(cbs[k], rts[k], ss[k], mk, stage_pages(k))
        fold(cbs[k - 1], ss[k - 1], mk, None)
    fold(cbs[nblk - 1], ss[nblk - 1], mk, None)

    @pl.when(j == pl.num_programs(1) - 1)
    def _():
        def stage_new(n0, n):
            pad = jnp.zeros((n - nq, KV_LORA), F32)
            cbs[0][0:n, :] = jnp.concatenate([cn_ref[...], pad], axis=0).astype(BF16)
            rn = rnt_ref[...]
            rts[0][0:ROPE_DIM, 0:n] = rn.astype(BF16)
            return jnp.sum(rn * rn, axis=0, keepdims=True)

        scores(cbs[0], rts[0], ss[0], PAGE_SIZE, stage_new)
        key_t = lax.broadcasted_iota(jnp.int32, (ncol, PAGE_SIZE), 1)
        qq = lax.broadcasted_iota(jnp.int32, (ncol, PAGE_SIZE), 0) % nq
        fold(cbs[0], ss[0], PAGE_SIZE, key_t <= qq)
        o_ref[...] = acc_ref[...] * (1.0 / l_ref[...])

    @pl.when(t == n_steps - 1)
    def _():
        for cp in page_copies(nxt, 1 - page_slot):
            cp.wait()


def _attn_s_call(page_table, qa, ckv_new, krt_new, cache_ckv, cache_krope_t, wkbt):
    n_seq, n_pages = page_table.shape
    nq = qa.shape[2]
    assert nq == SEQ_ROWS
    nblk = max(d for d in range(1, ATTN_S_BLOCKS + 1) if n_pages % d == 0)
    npb = max(d for d in range(1, ATTN_S_PAGES + 1) if (n_pages // nblk) % d == 0)
    n_pg = npb * nblk
    ncol = H_M * nq
    nkd = H_M * NOPE_DIM
    mk = npb * PAGE_SIZE

    slot_scratch = [
        pltpu.VMEM((mk, KV_LORA), BF16),
        pltpu.VMEM((LANES, mk), BF16),
        pltpu.VMEM((ncol, mk), F32),
    ]
    grid_spec = pltpu.PrefetchScalarGridSpec(
        num_scalar_prefetch=1,
        grid=(n_seq, n_pages // n_pg),
        in_specs=[
            pl.BlockSpec((None, H_M, nq, QA_W), lambda b, j, pt: (b, 0, 0, 0)),
            pl.BlockSpec((None, nq, KV_LORA), lambda b, j, pt: (b, 0, 0)),
            pl.BlockSpec((None, ROPE_DIM, PAGE_SIZE), lambda b, j, pt: (b, 0, 0)),
            pl.BlockSpec((nkd, KV_LORA), lambda b, j, pt: (0, 0)),
            pl.BlockSpec(memory_space=pl.ANY),
            pl.BlockSpec(memory_space=pl.ANY),
        ],
        out_specs=pl.BlockSpec((None, ncol, KV_LORA), lambda b, j, pt: (b, 0, 0)),
        scratch_shapes=[
            pltpu.VMEM((nkd + ncol, KV_LORA), BF16),
            pltpu.VMEM((ncol, LANES), BF16),
        ] + slot_scratch * nblk + [
            pltpu.VMEM((ncol, 1), F32),
            pltpu.VMEM((ncol, 1), F32),
            pltpu.VMEM((ncol, KV_LORA), F32),
            pltpu.VMEM((2, n_pg, PAGE_SIZE, KV_LORA), F32),
            pltpu.VMEM((2, n_pg, ROPE_DIM, PAGE_SIZE), F32),
            pltpu.SemaphoreType.DMA((2, 2)),
        ],
    )
    return pl.pallas_call(
        functools.partial(_attn_s_kernel, npb=npb, nblk=nblk),
        grid_spec=grid_spec,
        out_shape=jax.ShapeDtypeStruct((n_seq, ncol, KV_LORA), F32),
        compiler_params=_params(("arbitrary", "arbitrary"), 56),
        name="attn_s",
    )(page_table.reshape(-1), qa, ckv_new, krt_new, wkbt, cache_ckv, cache_krope_t)


def _vexp_kernel(ctx_ref, wv_ref, o_ref):
    nb, nq, _ = ctx_ref.shape
    o_ref[...] = _dot(ctx_ref[...].reshape(nb * nq, KV_LORA).astype(BF16), wv_ref[...])


def _vexp_call(ctx, wvb):
    n_seq, _, nq, _ = ctx.shape
    return pl.pallas_call(
        _vexp_kernel,
        grid=(H_M,),
        in_specs=[
            pl.BlockSpec((n_seq, None, nq, KV_LORA), lambda h: (0, h, 0, 0)),
            pl.BlockSpec((None, KV_LORA, V_DIM), lambda h: (h, 0, 0)),
        ],
        out_specs=pl.BlockSpec((n_seq * nq, V_DIM), lambda h: (0, h)),
        out_shape=jax.ShapeDtypeStruct((n_seq * nq, D_M), F32),
        compiler_params=_params(("arbitrary",), 32),
        name="vexp",
    )(ctx, wvb)


def _mixout_kernel(a_ref, m_ref, ga_ref, gm_ref, x_ref, gt_ref, w_ref, o_ref, lhs_ref):
    @pl.when(pl.program_id(1) == 0)
    def _():
        def body(i, rows):
            a = a_ref[rows, :]
            lhs_ref[rows, :D_A] = (a * _rms_scale(a, D_A) * ga_ref[...]).astype(BF16)
            m = m_ref[rows, :]
            lhs_ref[rows, D_A:] = (m * _rms_scale(m, D_M) * gm_ref[...]).astype(BF16)

        _row_loop(a_ref.shape[0], body)

    y = _dot(lhs_ref[...], w_ref[...])
    if gt_ref.shape[0] == 1:
        o_ref[...] = x_ref[...] + gt_ref[...] * y
    else:
        o_ref[...] = y

        def body(i, rows):
            o_ref[rows, :] = x_ref[rows, :] + _mod_rows(gt_ref, i) * o_ref[rows, :]

        _row_loop(o_ref.shape[0], body)


def _mixout_call(tok, a, m, g_a, g_m, x, w_out):
    tm = tok.tile(512)
    tn = 512
    gt_spec = tok.mod_spec(tm, 5, width=tn, col=lambda i, j: j)
    return pl.pallas_call(
        _mixout_kernel,
        grid=(tok.n_tok // tm, D_MODEL // tn),
        in_specs=[
            pl.BlockSpec((tm, D_A), lambda i, j: (i, 0)),
            pl.BlockSpec((tm, D_M), lambda i, j: (i, 0)),
            pl.BlockSpec((1, D_A), lambda i, j: (0, 0)),
            pl.BlockSpec((1, D_M), lambda i, j: (0, 0)),
            pl.BlockSpec((tm, tn), lambda i, j: (i, j)),
            gt_spec,
            pl.BlockSpec((D_A + D_M, tn), lambda i, j: (0, j)),
        ],
        out_specs=pl.BlockSpec((tm, tn), lambda i, j: (i, j)),
        out_shape=jax.ShapeDtypeStruct((tok.n_tok, D_MODEL), F32),
        scratch_shapes=[pltpu.VMEM((tm, D_A + D_M), BF16)],
        compiler_params=_params(("arbitrary", "arbitrary"), 48),
        name="mixout",
    )(a, m, g_a, g_m, x, tok.mod, w_out)


def _rope_table(pos):
    inv_freq = ROPE_THETA ** (-jnp.arange(HALF_ROPE, dtype=F32) / HALF_ROPE)
    ang = pos.astype(F32)[:, None] * inv_freq[None, :]
    cos, sin = jnp.cos(ang), jnp.sin(ang)
    return jnp.concatenate([cos, cos, sin, sin], axis=-1)


def _dup_rope_gain(g):
    g1, g2 = g[NOPE_DIM:NOPE_DIM + HALF_ROPE], g[NOPE_DIM + HALF_ROPE:]
    return jnp.concatenate([g1, g2, g1, g2])[None, :]


def _layer(tok, x, pos, p, attn_fn):
    x = _ffn_call(tok, x, 0, p["g_norm1"], p["wg1"], p["wu1"], p["wd1"])
    uv = _uv_call(tok, x, p["g_norm2"], p["wu"], p["wv"], p["g_v_a"],
                  p["ws_s"] if tok.sample else p["ws_p"], p["bs_s"] if tok.sample else p["bs_p"])
    a_out = uv[0]
    tk = _rope_table(pos)
    cq, ckv, krd = _qkr_call(tok, x, p["g_norm2"], p["wb"], p["g_q_a"], p["g_kv_a"], tk)
    m_out = attn_fn(cq, ckv, krd, tk)
    x = _mixout_call(tok, a_out, m_out, p["g_out_a"], p["g_out_m"], x, p["w_out"])
    x = _ffn_call(tok, x, 6, p["g_norm3"], p["wg2"], p["wu2"], p["wd2"])
    return x, ckv, krd[:, :ROPE_DIM], (uv[1] if tok.sample else None)


def kernel(x_prompt, x_sample, cache_ckv, cache_krope, page_table, c_prompt, c_sample, w_ada, b_ada, g_norm1, w_ff1_gate, w_ff1_up, w_ff1_down, g_norm2, w_in, w_s, b_s, g_v_a, g_q_a, w_q_b, g_kv_a, w_kv_b, g_q_norm, g_k_norm, g_out_a, g_out_m, w_out, g_norm3, w_ff2_gate, w_ff2_up, w_ff2_down):
    n_seq_p, seq_len, _ = x_prompt.shape
    n_seq_s, dec_len, _ = x_sample.shape
    n_pages = page_table.shape[1]
    past = n_pages * PAGE_SIZE
    depth = w_ada.shape[0]
    assert depth == 1 and dec_len == 8

    w_in0 = w_in[0]
    o = 2 * D_A
    w_kr = w_in0[:, o + Q_LORA + KV_LORA:]
    w1, w2 = w_kr[:, :HALF_ROPE], w_kr[:, HALF_ROPE:]
    wq = w_q_b[0]
    qn, q1, q2 = wq[..., :NOPE_DIM], wq[..., NOPE_DIM:NOPE_DIM + HALF_ROPE], wq[..., NOPE_DIM + HALF_ROPE:]
    wkv = w_kv_b[0]
    wk_n, wv_n = wkv[..., :NOPE_DIM], wkv[..., NOPE_DIM:]
    gq, gk = g_q_norm[0], g_k_norm[0]
    tril = jnp.tril(jnp.ones((CHUNK, CHUNK), dtype=bool))
    ws_t = jnp.where(tril[None], w_s[0], 0.0)
    p = {
        "g_norm1": g_norm1, "g_norm2": g_norm2, "g_norm3": g_norm3,
        "wg1": w_ff1_gate[0].astype(BF16), "wu1": w_ff1_up[0].astype(BF16), "wd1": w_ff1_down[0].astype(BF16),
        "wg2": w_ff2_gate[0].astype(BF16), "wu2": w_ff2_up[0].astype(BF16), "wd2": w_ff2_down[0].astype(BF16),
        "wu": w_in0[:, :D_A].astype(BF16), "wv": w_in0[:, D_A:o].astype(BF16),
        "wb": jnp.concatenate([w_in0[:, o:o + Q_LORA + KV_LORA], w1, w2, -w2, w1], axis=1).astype(BF16),
        "g_v_a": g_v_a, "g_q_a": g_q_a, "g_kv_a": g_kv_a,
        "ws_p": ws_t.astype(BF16),
        "bs_p": jnp.broadcast_to(b_s[0][:, :, None], (H_A, CHUNK, HEAD_A)),
        "ws_s": jnp.repeat(ws_t[:, :dec_len, :dec_len].transpose(2, 1, 0), HEAD_A, axis=2),
        "bs_s": jnp.repeat(b_s[0][:, :dec_len].T, HEAD_A, axis=1),
        "g_out_a": g_out_a, "g_out_m": g_out_m, "w_out": w_out[0].astype(BF16),
    }
    wqb = jnp.concatenate([qn, q1, q2, -q2, q1], axis=-1).reshape(Q_LORA, H_M * QH_W).astype(BF16)
    wkvb = jnp.concatenate([wk_n.reshape(KV_LORA, -1), wv_n.reshape(KV_LORA, -1)], axis=1).astype(BF16)
    wkbt =wk_n.transpose(1, 2, 0).astype(BF16)
    wvb = wv_n.transpose(1, 0, 2).astype(BF16)
    gq_dup, gk_dup = _dup_rope_gain(gq), _dup_rope_gain(gk)
    gq_n, gk_n = gq[None, :NOPE_DIM], gk[None, :NOPE_DIM]

    n_c = n_seq_p + n_seq_s
    pad = (-n_c) % 8
    c_all = jnp.concatenate([c_prompt, c_sample, jnp.zeros((pad, D_MODEL), F32)], axis=0)
    mod = _mod_call(c_all, w_ada[0], b_ada)
    mod_p = mod[:n_seq_p].reshape(n_seq_p * N_MOD, 1, D_MODEL)
    mod_s = mod[n_seq_p:n_c]

    tok_p = _Tokens(False, n_seq_p * seq_len, seq_len, mod_p)

    def attn_prompt(cq, ckv, krd, tk):
        q, k, v = _qpost_call(tok_p, cq, ckv, krd, tk * gq_dup, gq_n, gk_n, gk_dup, wqb, wkvb, None)
        return _attn_p_call(q, k, v, n_seq_p, seq_len)

    yp, ckv_p, kr_p, _ = _layer(tok_p, x_prompt.reshape(-1, D_MODEL), jnp.arange(seq_len), p, attn_prompt)

    tok_s = _Tokens(True, n_seq_s * dec_len, dec_len, mod_s)
    cache_krope_t = jnp.swapaxes(cache_krope, 2, 3)

    def attn_sample(cq, ckv, krd, tk):
        qa = _qpost_call(tok_s, cq, None, None, tk * (gq_dup * gk_dup), gq_n * gk_n, None, None, wqb, None, wkbt)
        krt_new = jnp.swapaxes(krd[:, :ROPE_DIM].reshape(n_seq_s, dec_len, ROPE_DIM), 1, 2)
        krt_new = jnp.pad(krt_new, ((0, 0), (0, 0), (0, PAGE_SIZE - dec_len)))
        ctx = _attn_s_call(page_table, qa, ckv.reshape(n_seq_s, dec_len, KV_LORA), krt_new,
                           cache_ckv, cache_krope_t, wkbt.reshape(H_M * NOPE_DIM, KV_LORA))
        return _vexp_call(ctx.reshape(n_seq_s, H_M, dec_len, KV_LORA), wvb)

    pos_s = jnp.tile(past + jnp.arange(dec_len), n_seq_s)
    ys, ckv_s, kr_s, va_s = _layer(tok_s, x_sample.reshape(-1, D_MODEL), pos_s, p, attn_sample)

    return (
        yp.reshape(n_seq_p, seq_len, D_MODEL),
        ys.reshape(n_seq_s, dec_len, D_MODEL),
        ckv_p.reshape(1, n_seq_p, seq_len, KV_LORA),
        kr_p.reshape(1, n_seq_p, seq_len, ROPE_DIM),
        ckv_s.reshape(1, n_seq_s, dec_len, KV_LORA),
        kr_s.reshape(1, n_seq_s, dec_len, ROPE_DIM),
        va_s.reshape(1, n_seq_s, dec_len, H_A, HEAD_A),
    )
```
